```python
import math
import jax
import jax.numpy as jnp
from jax import lax
import numpy as np

D_MODEL = 1024
BATCH = 4
SEQ = 4096
DEPTH = 1
DEC_BATCH = 128
DEC_SEQ = 1
PAST_LEN = 8192
PAGE_SIZE = 128

HEAD_DIM = 64
N_HEADS_A = 8
N_HEADS_B = 8
N_IDX_HEADS = 8
IDX_DIM = 64
TOPK_MAX = 256
ROPE_THETA = 10000.0
Q_BLOCK = 128
RMS_EPS = 1e-6
D_FF = ((8 * D_MODEL // 3 + 255) // 256) * 256
W_A = N_HEADS_A * HEAD_DIM
W_B = N_HEADS_B * HEAD_DIM
W_IN_COLS = 3 * W_A + N_IDX_HEADS * IDX_DIM + IDX_DIM + N_IDX_HEADS + 3 * W_B + 2 * D_MODEL
POOL_NUM = 5
POOL_DEN = 4

kernel_name = "hybrid_dsa_stickbreaking_decode_step"


def _rms_norm(x, g):
    xf = x.astype(jnp.float32)
    y = xf * lax.rsqrt(jnp.mean(xf * xf, axis=-1, keepdims=True) + RMS_EPS)
    return (y * g.astype(jnp.float32)).astype(x.dtype)


def _modulate(x, g, shift, scale):
    return _rms_norm(x, g) * (1.0 + scale[:, None, :]) + shift[:, None, :]


def _rope(x, pos):
    half = x.shape[-1] // 2
    freqs = ROPE_THETA ** (-jnp.arange(half, dtype=jnp.float32) / half)
    ang = pos.astype(jnp.float32)[:, None] * freqs[None, :]
    cos = jnp.cos(ang)[None, :, None, :]
    sin = jnp.sin(ang)[None, :, None, :]
    xf = x.astype(jnp.float32)
    x1, x2 = xf[..., :half], xf[..., half:]
    return jnp.concatenate([x1 * cos - x2 * sin, x1 * sin + x2 * cos], axis=-1).astype(x.dtype)


def _split_projection(proj):
    sizes = (W_A, W_A, W_A, N_IDX_HEADS * IDX_DIM, IDX_DIM, N_IDX_HEADS, W_B, W_B, W_B, D_MODEL, D_MODEL)
    offsets = []
    acc = 0
    for s in sizes[:-1]:
        acc += s
        offsets.append(acc)
    return jnp.split(proj, offsets, axis=-1)


def _indexer_scores(q_i, w_i, k_i):
    s = jnp.einsum("bthd,bsd->bths", q_i.astype(jnp.float32), k_i.astype(jnp.float32)) * (IDX_DIM ** -0.5)
    return jnp.einsum("bths,bth->bts", jax.nn.relu(s), w_i.astype(jnp.float32)) * (N_IDX_HEADS ** -0.5)


def _sparse_attend(q, kg, vg, valid):
    logits = jnp.einsum("bthd,btkhd->bthk", q.astype(jnp.float32), kg.astype(jnp.float32)) / math.sqrt(q.shape[-1])
    logits = jnp.where(valid[:, :, None, :], logits, -jnp.inf)
    p = jax.nn.softmax(logits, axis=-1)
    return jnp.einsum("bthk,btkhd->bthd", p, vg.astype(jnp.float32)).astype(q.dtype)


def _dsa_prompt(q, k, v, q_i, w_i, k_i):
    bsz, seq, heads, dh = q.shape
    topk = min(TOPK_MAX, seq // 4)
    kpos = jnp.arange(seq)
    bidx = jnp.arange(bsz)[:, None, None]

    def block(t0):
        qb = lax.dynamic_slice_in_dim(q, t0, Q_BLOCK, axis=1)
        qib = lax.dynamic_slice_in_dim(q_i, t0, Q_BLOCK, axis=1)
        wib = lax.dynamic_slice_in_dim(w_i, t0, Q_BLOCK, axis=1)
        scores = _indexer_scores(qib, wib, k_i)
        qpos = t0 + jnp.arange(Q_BLOCK)
        causal = kpos[None, :] <= qpos[:, None]
        scores = jnp.where(causal[None], scores, -jnp.inf)
        vals, idx = lax.top_k(scores, topk)
        return _sparse_attend(qb, k[bidx, idx], v[bidx, idx], jnp.isfinite(vals))

    out = lax.map(block, jnp.arange(seq // Q_BLOCK, dtype=jnp.int32) * Q_BLOCK)
    return jnp.moveaxis(out, 0, 1).reshape(bsz, seq, heads, dh)


def _dsa_sample(q, k_new, v_new, q_i, w_i, ki_new, pool_k, pool_v, pool_ki, layer, page_table):
    dbs, t_new, heads, dh = q.shape
    past = page_table.shape[1] * PAGE_SIZE
    n_keys = past + t_new
    topk = min(TOPK_MAX, n_keys // 4)
    ki_past = pool_ki[layer, page_table].reshape(dbs, past, IDX_DIM)
    ki_all = jnp.concatenate([ki_past, ki_new], axis=1)
    scores = _indexer_scores(q_i, w_i, ki_all)
    causal = jnp.arange(n_keys)[None, :] <= (past + jnp.arange(t_new))[:, None]
    scores = jnp.where(causal[None], scores, -jnp.inf)
    vals, idx = lax.top_k(scores, topk)
    bidx = jnp.arange(dbs)[:, None, None]
    in_past = (idx < past)[..., None, None]
    pidx = jnp.minimum(idx, past - 1)
    phys = page_table[bidx, pidx // PAGE_SIZE]
    off = pidx % PAGE_SIZE
    nidx = jnp.clip(idx - past, 0, t_new - 1)
    kg = jnp.where(in_past, pool_k[layer, phys, off], k_new[bidx, nidx])
    vg = jnp.where(in_past, pool_v[layer, phys, off], v_new[bidx, nidx])
    return _sparse_attend(q, kg, vg, jnp.isfinite(vals))


def _stick_breaking_prompt(q, k, v):
    bsz, seq, heads, dh = q.shape
    kpos = jnp.arange(seq)
    kf = k.astype(jnp.float32)
    vf = v.astype(jnp.float32)

    def block(t0):
        qb = lax.dynamic_slice_in_dim(q, t0, Q_BLOCK, axis=1).astype(jnp.float32)
        z = jnp.einsum("bthd,bshd->bths", qb, kf) / math.sqrt(dh)
        qpos = t0 + jnp.arange(Q_BLOCK)
        strict = (kpos[None, :] < qpos[:, None])[None, :, None, :]
        log_not = jnp.where(strict, jax.nn.log_sigmoid(-z), 0.0)
        between = lax.cumsum(log_not, axis=3, reverse=True) - log_not
        a = jnp.where(strict, jnp.exp(jax.nn.log_sigmoid(z) + between), 0.0)
        return jnp.einsum("bths,bshd->bthd", a, vf).astype(q.dtype)

    out = lax.map(block, jnp.arange(seq // Q_BLOCK, dtype=jnp.int32) * Q_BLOCK)
    return jnp.moveaxis(out, 0, 1).reshape(bsz, seq, heads, dh)


def _stick_breaking_sample(q, k_new, v_new, pool_k, pool_v, layer, page_table):
    dbs, t_new, heads, dh = q.shape
    qf = q.astype(jnp.float32)
    z = jnp.einsum("bthd,bshd->bths", qf, k_new.astype(jnp.float32)) / math.sqrt(dh)
    strict = (jnp.arange(t_new)[None, :] < jnp.arange(t_new)[:, None])[None, :, None, :]
    log_not = jnp.where(strict, jax.nn.log_sigmoid(-z), 0.0)
    between = lax.cumsum(log_not, axis=3, reverse=True) - log_not
    a = jnp.where(strict, jnp.exp(jax.nn.log_sigmoid(z) + between), 0.0)
    out0 = jnp.einsum("bths,bshd->bthd", a, v_new.astype(jnp.float32))
    carry0 = jnp.sum(log_not, axis=-1)

    def page_step(carry, phys):
        acc_log, out = carry
        kp = pool_k[layer, phys].astype(jnp.float32)
        vp = pool_v[layer, phys].astype(jnp.float32)
        zp = jnp.einsum("bthd,bshd->bths", qf, kp) / math.sqrt(dh)
        ln = jax.nn.log_sigmoid(-zp)
        bet = lax.cumsum(ln, axis=3, reverse=True) - ln + acc_log[..., None]
        ap = jnp.exp(jax.nn.log_sigmoid(zp) + bet)
        out = out + jnp.einsum("bths,bshd->bthd", ap, vp)
        return (acc_log + jnp.sum(ln, axis=-1), out), None

    (_, out), _ = lax.scan(page_step, (carry0, out0), page_table.T[::-1])
    return out.astype(q.dtype)


def _layer(x, c, pos, attend, w_ada, b_ada, g_pre_mix, g_post_mix, w_in, w_o_a, w_o_b, w_out,
           g_pre_ffn, g_post_ffn, w_ffn_in, w_ffn_out):
    bsz, t = x.shape[:2]
    mod = jax.nn.silu(c) @ w_ada + b_ada
    shift_m, scale_m, gate_m, shift_f, scale_f, gate_f = jnp.split(mod, 6, axis=-1)
    h = _modulate(x, g_pre_mix, shift_m, scale_m)
    q_a, k_a, v_a, q_i, k_i, w_i, q_b, k_b, v_b, gt_a, gt_b = _split_projection(h @ w_in)
    q_a = _rope(q_a.reshape(bsz, t, N_HEADS_A, HEAD_DIM), pos)
    k_a = _rope(k_a.reshape(bsz, t, N_HEADS_A, HEAD_DIM), pos)
    v_a = v_a.reshape(bsz, t, N_HEADS_A, HEAD_DIM)
    q_i = _rope(q_i.reshape(bsz, t, N_IDX_HEADS, IDX_DIM), pos)
    k_i = _rope(k_i.reshape(bsz, t, 1, IDX_DIM), pos)[:, :, 0]
    q_b = q_b.reshape(bsz, t, N_HEADS_B, HEAD_DIM)
    k_b = k_b.reshape(bsz, t, N_HEADS_B, HEAD_DIM)
    v_b = v_b.reshape(bsz, t, N_HEADS_B, HEAD_DIM)
    o_a, o_b = attend(q_a, k_a, v_a, q_i, k_i, w_i, q_b, k_b, v_b)
    merged = (jax.nn.sigmoid(gt_a) * (o_a.reshape(bsz, t, W_A) @ w_o_a)
              + jax.nn.sigmoid(gt_b) * (o_b.reshape(bsz, t, W_B) @ w_o_b))
    x = x + gate_m[:, None, :] * _rms_norm(merged @ w_out, g_post_mix)
    h = _modulate(x, g_pre_ffn, shift_f, scale_f)
    ff_gate, ff_up = jnp.split(h @ w_ffn_in, 2, axis=-1)
    x = x + gate_f[:, None, :] * _rms_norm((jax.nn.silu(ff_gate) * ff_up) @ w_ffn_out, g_post_ffn)
    return x, (k_a, v_a, k_i, k_b, v_b)


def setup_inputs(seed: int = 0) -> dict:
    key = jax.random.key(seed)
    ks = jax.random.split(key, 24)
    n_pages = PAST_LEN // PAGE_SIZE
    n_used = DEC_BATCH * n_pages
    n_pool = (n_used * POOL_NUM) // POOL_DEN
    page_table = jax.random.permutation(ks[9], n_pool)[:n_used].reshape(DEC_BATCH, n_pages).astype(jnp.int32)

    def nrm(k, shape, scale=1.0):
        return scale * jax.random.normal(k, shape, jnp.float32)

    def dense(k, fan_in, shape, gain=1.0):
        return nrm(k, shape, gain * fan_in ** -0.5)

    def norm_gain(k):
        return 1.0 + nrm(k, (DEPTH, D_MODEL), 0.05)

    return {
        "x_prompt": nrm(ks[0], (BATCH, SEQ, D_MODEL)),
        "x_sample": nrm(ks[1], (DEC_BATCH, DEC_SEQ, D_MODEL)),
        "c_prompt": nrm(ks[2], (BATCH, D_MODEL)),
        "c_sample": nrm(ks[3], (DEC_BATCH, D_MODEL)),
        "cache_k_a": nrm(ks[4], (DEPTH, n_pool, PAGE_SIZE, N_HEADS_A, HEAD_DIM)),
        "cache_v_a": nrm(ks[5], (DEPTH, n_pool, PAGE_SIZE, N_HEADS_A, HEAD_DIM)),
        "cache_k_idx": nrm(ks[6], (DEPTH, n_pool, PAGE_SIZE, IDX_DIM)),
        "cache_k_b": nrm(ks[7], (DEPTH, n_pool, PAGE_SIZE, N_HEADS_B, HEAD_DIM)),
        "cache_v_b": nrm(ks[8], (DEPTH, n_pool, PAGE_SIZE, N_HEADS_B, HEAD_DIM)),
        "page_table": page_table,
        "w_ada": dense(ks[10], D_MODEL, (DEPTH, D_MODEL, 6 * D_MODEL), 0.5),
        "b_ada": nrm(ks[11], (DEPTH, 6 * D_MODEL), 0.01),
        "g_pre_mix": norm_gain(ks[12]),
        "g_post_mix": norm_gain(ks[13]),
        "w_in": dense(ks[14], D_MODEL, (DEPTH, D_MODEL, W_IN_COLS)),
        "w_o_a": dense(ks[15], W_A, (DEPTH, W_A, D_MODEL)),
        "w_o_b": dense(ks[16], W_B, (DEPTH, W_B, D_MODEL)),
        "w_out": dense(ks[17], D_MODEL, (DEPTH, D_MODEL, D_MODEL)),
        "g_pre_ffn": norm_gain(ks[18]),
        "g_post_ffn": norm_gain(ks[19]),
        "w_ffn_in": dense(ks[20], D_MODEL, (DEPTH, D_MODEL, 2 * D_FF)),
        "w_ffn_out": dense(ks[21], D_FF, (DEPTH, D_FF, D_MODEL)),
    }


def reference(x_prompt, x_sample, c_prompt, c_sample, cache_k_a, cache_v_a, cache_k_idx, cache_k_b, cache_v_b,
              page_table, w_ada, b_ada, g_pre_mix, g_post_mix, w_in, w_o_a, w_o_b, w_out,
              g_pre_ffn, g_post_ffn, w_ffn_in, w_ffn_out):
    seq = x_prompt.shape[1]
    dec_seq = x_sample.shape[1]
    past = page_table.shape[1] * PAGE_SIZE
    pos_prompt = jnp.arange(seq, dtype=jnp.int32)
    pos_sample = past + jnp.arange(dec_seq, dtype=jnp.int32)

    def attend_prompt(q_a, k_a, v_a, q_i, k_i, w_i, q_b, k_b, v_b):
        return _dsa_prompt(q_a, k_a, v_a, q_i, w_i, k_i), _stick_breaking_prompt(q_b, k_b, v_b)

    yp, ys = x_prompt, x_sample
    rows_p, rows_s = [], []
    for layer in range(DEPTH):
        def attend_sample(q_a, k_a, v_a, q_i, k_i, w_i, q_b, k_b, v_b, layer=layer):
            o_a = _dsa_sample(q_a, k_a, v_a, q_i, w_i, k_i, cache_k_a, cache_v_a, cache_k_idx, layer, page_table)
            o_b = _stick_breaking_sample(q_b, k_b, v_b, cache_k_b, cache_v_b, layer, page_table)
            return o_a, o_b

        weights = (w_ada[layer], b_ada[layer], g_pre_mix[layer], g_post_mix[layer], w_in[layer], w_o_a[layer],
                   w_o_b[layer], w_out[layer], g_pre_ffn[layer], g_post_ffn[layer], w_ffn_in[layer], w_ffn_out[layer])
        yp, rp = _layer(yp, c_prompt, pos_prompt, attend_prompt, *weights)
        ys, rs = _layer(ys, c_sample, pos_sample, attend_sample, *weights)
        rows_p.append(rp)
        rows_s.append(rs)

    new_k_a_prompt = jnp.stack([r[0] for r in rows_p])
    new_v_a_prompt = jnp.stack([r[1] for r in rows_p])
    new_k_idx_prompt = jnp.stack([r[2] for r in rows_p])
    new_k_b_prompt = jnp.stack([r[3] for r in rows_p])
    new_v_b_prompt = jnp.stack([r[4] for r in rows_p])
    new_k_a_sample = jnp.stack([r[0] for r in rows_s])
    new_v_a_sample = jnp.stack([r[1] for r in rows_s])
    new_k_idx_sample = jnp.stack([r[2] for r in rows_s])
    new_k_b_sample = jnp.stack([r[3] for r in rows_s])
    new_v_b_sample = jnp.stack([r[4] for r in rows_s])
    return (yp, ys, new_k_a_prompt, new_v_a_prompt, new_k_idx_prompt, new_k_b_prompt, new_v_b_prompt,
            new_k_a_sample, new_v_a_sample, new_k_idx_sample, new_k_b_sample, new_v_b_sample)
```

```python
import functools

import jax
import jax.numpy as jnp
from jax import lax
from jax.experimental import pallas as pl
from jax.experimental.pallas import tpu as pltpu

F32 = jnp.float32
BF16 = jnp.bfloat16
I32 = jnp.int32

D_MODEL = 1024
HEAD_DIM = 64
N_HEADS = 8
W_HEADS = N_HEADS * HEAD_DIM
IDX_DIM = 64
TOPK_MAX = 256
ROPE_THETA = 10000.0
RMS_EPS = 1e-6
PAGE = 128

LANES = 128
INT_MIN = -(2**31)
NEG = -1e30
EXP_ZERO_BELOW = -104.0
VMEM_LIMIT = 56 * 1024 * 1024

TQ = 128
CK = 512
DSA_PAGES_PER_STEP = 8
IDX_PAGES_PER_STEP = 16

_SEC = {}
_off = 0
for _name, _w in (("qa", 512), ("ka", 512), ("va", 512), ("qi", 512), ("kiw", 128),
                  ("qb", 512), ("kb", 512), ("vb", 512), ("ga", 1024), ("gb", 1024)):
    _SEC[_name] = (_off, _off + _w)
    _off += _w
W_IN_PACKED = _off


def _nt_dot(a, b):
    return lax.dot_general(a, b, (((1,), (1,)), ((), ())), preferred_element_type=F32)


def _rms(x):
    return x * lax.rsqrt(jnp.mean(x * x, axis=-1, keepdims=True) + RMS_EPS)


def _mod_kernel(c_ref, w_ref, b_ref, o_ref):
    c = c_ref[...]
    s = c * (1.0 / (1.0 + jnp.exp(-c)))
    o_ref[...] = jnp.dot(s, w_ref[...], preferred_element_type=F32,
                         precision=lax.Precision.HIGHEST) + b_ref[...]


def _modulation(c_all, w_ada, b_ada):
    m = c_all.shape[0]
    n = w_ada.shape[1]
    tn = 512
    return pl.pallas_call(
        _mod_kernel,
        out_shape=jax.ShapeDtypeStruct((m, n), F32),
        grid=(n // tn,),
        in_specs=[pl.BlockSpec((m, D_MODEL), lambda j: (0, 0)),
                  pl.BlockSpec((D_MODEL, tn), lambda j: (0, j)),
                  pl.BlockSpec((1, tn), lambda j: (0, j))],
        out_specs=pl.BlockSpec((m, tn), lambda j: (0, j)),
        compiler_params=pltpu.CompilerParams(dimension_semantics=("arbitrary",)),
        name="adaln_mod",
    )(c_all, w_ada, b_ada)


def _proj_kernel(x_ref, g_ref, shift_ref, scale_ref, w_ref, cos_ref, sin_ref, cosk_ref, sink_ref,
                 qa_ref, kab_ref, vab_ref, qi_ref, kiwf_ref, kib_ref, qb_ref, kbb_ref, vbb_ref, ga_ref, gb_ref,
                 kat_ref, vat_ref, kit_ref, kbt_ref, vbt_ref):
    tm = x_ref.shape[0]
    h = (_rms(x_ref[...]) * g_ref[...] * (1.0 + scale_ref[0]) + shift_ref[0]).astype(BF16)

    def mm(name):
        a, b = _SEC[name]
        return jnp.dot(h, w_ref[:, a:b], preferred_element_type=F32)

    lane = lax.broadcasted_iota(I32, (tm, LANES), 1)
    first_half = (lane & 32) == 0

    def rope(p, cos, sin):
        outs = []
        for c in range(p.shape[1] // LANES):
            xc = p[:, c * LANES:(c + 1) * LANES]
            partner = jnp.where(first_half, pltpu.roll(xc, LANES - 32, 1), pltpu.roll(xc, 32, 1))
            outs.append(xc * cos + partner * sin)
        return outs[0] if len(outs) == 1 else jnp.concatenate(outs, axis=1)

    cos = cos_ref[...]
    sin = sin_ref[...]
    scale = HEAD_DIM ** -0.5
    qa_ref[...] = (rope(mm("qa"), cos, sin) * scale).astype(BF16)
    ka = rope(mm("ka"), cos, sin)
    kab_ref[...] = ka.astype(BF16)
    kat_ref[...] = ka.T
    va = mm("va")
    vab_ref[...] = va.astype(BF16)
    vat_ref[...] = va.T
    qi_ref[...] = (rope(mm("qi"), cos, sin) * (IDX_DIM ** -0.5)).astype(BF16)
    kiw = rope(mm("kiw"), cosk_ref[...], sink_ref[...])
    kiwf_ref[...] = kiw
    kib_ref[...] = kiw.astype(BF16)
    kit_ref[...] = kiw.T[:IDX_DIM, :]
    qb_ref[...] = (mm("qb") * scale).astype(BF16)
    kb = mm("kb")
    kbb_ref[...] = kb.astype(BF16)
    kbt_ref[...] = kb.T
    vb = mm("vb")
    vbb_ref[...] = vb.astype(BF16)
    vbt_ref[...] = vb.T
    ga = mm("ga")
    ga_ref[...] = (1.0 / (1.0 + jnp.exp(-ga))).astype(BF16)
    gb = mm("gb")
    gb_ref[...] = (1.0 / (1.0 + jnp.exp(-gb))).astype(BF16)


def _projection(x2d, g_pre, mod3, w_packed, tabs, tm, rows_per_mod, n_pos_tiles):
    rows = x2d.shape[0]
    n_tiles = rows // tm
    r_mod = mod3.shape[1]
    tiles_per_mod = rows_per_mod // tm
    groups = rows // rows_per_mod

    def row_spec(width):
        return pl.BlockSpec((tm, width), lambda i: (i, 0))

    def t_spec(width):
        return pl.BlockSpec((None, width, tm), lambda i: (i // tiles_per_mod, 0, i % tiles_per_mod))

    def mod_spec(chunk):
        return pl.BlockSpec((1, r_mod, D_MODEL), lambda i: (i // tiles_per_mod, 0, chunk))

    tab_spec = pl.BlockSpec((tm, LANES), lambda i: (i % n_pos_tiles, 0))
    outs = [("qa", 512, BF16), ("kab", 512, BF16), ("vab", 512, BF16),
            ("qi", 512, BF16), ("kiwf", 128, F32), ("kib", 128, BF16), ("qb", 512, BF16),
            ("kbb", 512, BF16), ("vbb", 512, BF16), ("ga", 1024, BF16), ("gb", 1024, BF16)]
    outs_t = [("kat", W_HEADS), ("vat", W_HEADS), ("kit", IDX_DIM), ("kbt", W_HEADS), ("vbt", W_HEADS)]
    res = pl.pallas_call(
        _proj_kernel,
        out_shape=[jax.ShapeDtypeStruct((rows, w), dt) for _, w, dt in outs]
        + [jax.ShapeDtypeStruct((groups, w, rows_per_mod), F32) for _, w in outs_t],
        grid=(n_tiles,),
        in_specs=[row_spec(D_MODEL),
                  pl.BlockSpec((1, D_MODEL), lambda i: (0, 0)),
                  mod_spec(0), mod_spec(1),
                  pl.BlockSpec((D_MODEL, W_IN_PACKED), lambda i: (0, 0)),
                  tab_spec, tab_spec, tab_spec, tab_spec],
        out_specs=[row_spec(w) for _, w, _ in outs] + [t_spec(w) for _, w in outs_t],
        compiler_params=pltpu.CompilerParams(dimension_semantics=("arbitrary",),
                                             vmem_limit_bytes=VMEM_LIMIT),
        name="in_proj",
    )(x2d, g_pre, mod3, mod3, w_packed, *tabs)
    return {n: r for n, r in zip([o[0] for o in outs] + [o[0] for o in outs_t], res)}


def _masked_scores(acc, valid):
    return jnp.where(valid, acc, -jnp.inf)


def _key_to_float(key):
    bits = jnp.where(key < 0, (key - 1) ^ jnp.int32(0x7FFFFFFF), key)
    return pltpu.bitcast(bits, F32)


def _lane_fold(x):
    out = x[:, :LANES]
    for q in range(1, x.shape[1] // LANES):
        out = out + x[:, q * LANES:(q + 1) * LANES]
    return out


def _select_topk(score_ref, bias_ref, n_chunks, k):
    rows = score_ref.shape[0]
    kf = float(k)

    def chunk_slice(c):
        return pl.ds(pl.multiple_of(c * CK, CK), CK)

    def count_ge(cand):
        def chunk(c, cnt):
            return cnt + _lane_fold(jnp.where(score_ref[:, chunk_slice(c)] >= cand, 1.0, 0.0))
        cnt = lax.fori_loop(0, n_chunks, chunk, jnp.zeros((rows, LANES), F32))
        return jnp.sum(cnt, axis=1, keepdims=True)

    def bit_step(it, tu):
        cand_u = tu | jnp.left_shift(jnp.int32(1), 31 - it)
        cnt = count_ge(_key_to_float(cand_u ^ jnp.int32(INT_MIN)))
        return jnp.where(cnt >= kf, cand_u, tu)

    tu = lax.fori_loop(0, 32, bit_step, jnp.zeros((rows, 1), I32))
    ts = jnp.where(tu == 0, -jnp.inf, _key_to_float(tu ^ jnp.int32(INT_MIN)))

    def tally(c, carry):
        cge, cgt = carry
        sc = score_ref[:, chunk_slice(c)]
        ge = sc >= ts
        bias_ref[:, chunk_slice(c)] = jnp.where(ge & (sc > -jnp.inf), 0.0, NEG)
        return (cge + _lane_fold(jnp.where(ge, 1.0, 0.0)), cgt + _lane_fold(jnp.where(sc > ts, 1.0, 0.0)))

    zero = jnp.zeros((rows, LANES), F32)
    cge, cgt = lax.fori_loop(0, n_chunks, tally, (zero, zero))
    cnt_ge = jnp.sum(cge, axis=1, keepdims=True)
    cnt_gt = jnp.sum(cgt, axis=1, keepdims=True)
    tie = (cnt_ge > kf) & (ts > -jnp.inf)

    @pl.when(jnp.max(jnp.where(tie, 1.0, 0.0)) > 0.0)
    def _():
        need = kf - cnt_gt
        r = lax.broadcasted_iota(I32, (LANES, LANES), 0)
        c_ = lax.broadcasted_iota(I32, (LANES, LANES), 1)
        before = jnp.where(r < c_, 1.0, 0.0).astype(BF16)

        def chunk(c, run):
            sl = pl.ds(pl.multiple_of(c * LANES, LANES), LANES)
            sc = score_ref[:, sl]
            eq = jnp.where(sc == ts, 1.0, 0.0)
            rank = jnp.dot(eq.astype(BF16), before, preferred_element_type=F32) + run
            take = (sc > ts) | ((sc == ts) & (rank < need))
            bias_ref[:, sl] = jnp.where(tie, jnp.where(take, 0.0, NEG), bias_ref[:, sl])
            return run + jnp.sum(eq, axis=1, keepdims=True)

        lax.fori_loop(0, n_chunks * (CK // LANES), chunk, jnp.zeros((rows, 1), F32))


def _dsa_prompt_kernel(qi_ref, kiwq_ref, ki_ref, qa_ref, ka_ref, va_ref, o_ref, score_ref, bias_ref):
    i = pl.program_id(1)
    n_chunks = i // (CK // TQ) + 1
    qi = qi_ref[...]
    w = kiwq_ref[:, IDX_DIM:IDX_DIM + N_HEADS] * (N_HEADS ** -0.5)
    qih = [qi[:, h * IDX_DIM:(h + 1) * IDX_DIM] for h in range(N_HEADS)]
    wh = [w[:, h:h + 1] for h in range(N_HEADS)]
    t_idx = i * TQ + lax.broadcasted_iota(I32, (TQ, CK), 0)
    col = lax.broadcasted_iota(I32, (TQ, CK), 1)

    def score_chunk(c, _):
        start = pl.multiple_of(c * CK, CK)
        kic = ki_ref[pl.ds(start, CK), :IDX_DIM]
        acc = jnp.zeros((TQ, CK), F32)
        for h in range(N_HEADS):
            acc = acc + jnp.maximum(_nt_dot(qih[h], kic), 0.0) * wh[h]
        score_ref[:, pl.ds(start, CK)] = _masked_scores(acc, (col + start) <= t_idx)
        return 0

    lax.fori_loop(0, n_chunks, score_chunk, 0)
    _select_topk(score_ref, bias_ref, n_chunks, TOPK_MAX)

    qa = qa_ref[...]
    qah = [qa[:, h * HEAD_DIM:(h + 1) * HEAD_DIM] for h in range(N_HEADS)]

    def att_chunk(c, carry):
        ms, ls, accs = carry
        start = pl.multiple_of(c * CK, CK)
        bias = bias_ref[:, pl.ds(start, CK)]
        kc = ka_ref[pl.ds(start, CK), :]
        vc = va_ref[pl.ds(start, CK), :]
        ms2, ls2, accs2 = [], [], []
        for h in range(N_HEADS):
            hs = slice(h * HEAD_DIM, (h + 1) * HEAD_DIM)
            lg = _nt_dot(qah[h], kc[:, hs]) + bias
            m_new = jnp.maximum(ms[h], jnp.max(lg, axis=1, keepdims=True))
            alpha = jnp.exp(ms[h] - m_new)
            p = jnp.exp(lg - m_new)
            ms2.append(m_new)
            ls2.append(alpha * ls[h] + jnp.sum(p, axis=1, keepdims=True))
            accs2.append(alpha * accs[h] + jnp.dot(p.astype(BF16), vc[:, hs], preferred_element_type=F32))
        return tuple(ms2), tuple(ls2), tuple(accs2)

    init = (tuple(jnp.full((TQ, 1), NEG, F32) for _ in range(N_HEADS)),
            tuple(jnp.zeros((TQ, 1), F32) for _ in range(N_HEADS)),
            tuple(jnp.zeros((TQ, HEAD_DIM), F32) for _ in range(N_HEADS)))
    _, ls, accs = lax.fori_loop(0, n_chunks, att_chunk, init)
    o_ref[...] = jnp.concatenate([accs[h] / ls[h] for h in range(N_HEADS)], axis=1).astype(o_ref.dtype)


def _dsa_prompt(p, bsz, seq):
    nq = seq // TQ

    def q_spec(width):
        return pl.BlockSpec((TQ, width), lambda b, i: (b * nq + i, 0))

    def kv_spec(width):
        return pl.BlockSpec((None, seq, width), lambda b, i: (b, 0, 0))

    return pl.pallas_call(
        _dsa_prompt_kernel,
        out_shape=jax.ShapeDtypeStruct((bsz * seq, W_HEADS), BF16),
        grid=(bsz, nq),
        in_specs=[q_spec(W_HEADS), q_spec(LANES), kv_spec(LANES), q_spec(W_HEADS), kv_spec(W_HEADS),
                  kv_spec(W_HEADS)],
        out_specs=q_spec(W_HEADS),
        scratch_shapes=[pltpu.VMEM((TQ, seq), F32), pltpu.VMEM((TQ, seq), F32)],
        compiler_params=pltpu.CompilerParams(dimension_semantics=("arbitrary", "arbitrary"),
                                             vmem_limit_bytes=VMEM_LIMIT),
        name="dsa_prompt",
    )(p["qi"], p["kiwf"], p["kib"].reshape(bsz, seq, LANES), p["qa"],
      p["kab"].reshape(bsz, seq, W_HEADS), p["vab"].reshape(bsz, seq, W_HEADS))


def _log_sigmoid_neg(z):
    return jnp.minimum(-z, 0.0) - jnp.log1p(jnp.exp(-jnp.abs(z)))


def _suffix_sum_excl(lg, after):
    hi = lg.astype(BF16)
    lo = (lg - hi.astype(F32)).astype(BF16)
    return (jnp.dot(hi, after, preferred_element_type=F32)
            + jnp.dot(lo, after, preferred_element_type=F32))


def _after_matrix(n):
    r = lax.broadcasted_iota(I32, (n, n), 0)
    c = lax.broadcasted_iota(I32, (n, n), 1)
    return jnp.where(r > c, 1.0, 0.0).astype(BF16)


def _sb_prompt_kernel(qb_ref, kb_ref, vb_ref, o_ref):
    i = pl.program_id(1)
    after = _after_matrix(TQ)
    t_idx = i * TQ + lax.broadcasted_iota(I32, (TQ, TQ), 0)
    col = lax.broadcasted_iota(I32, (TQ, TQ), 1)
    qb = qb_ref[...]
    qbh = [qb[:, h * HEAD_DIM:(h + 1) * HEAD_DIM] for h in range(N_HEADS)]

    def body(carry):
        j, logs, outs = carry
        start = pl.multiple_of(j * TQ, TQ)
        strict = (col + start) < t_idx
        kblk = kb_ref[pl.ds(start, TQ), :]
        vblk = vb_ref[pl.ds(start, TQ), :]
        logs2, outs2 = [], []
        for h in range(N_HEADS):
            hs = slice(h * HEAD_DIM, (h + 1) * HEAD_DIM)
            z = _nt_dot(qbh[h], kblk[:, hs])
            lraw = _log_sigmoid_neg(z)
            lg = jnp.where(strict, lraw, 0.0)
            between = _suffix_sum_excl(lg, after) + logs[h]
            a = jnp.where(strict, jnp.exp(z + lraw + between), 0.0)
            outs2.append(outs[h] + jnp.dot(a.astype(BF16), vblk[:, hs], preferred_element_type=F32))
            logs2.append(logs[h] + jnp.sum(lg, axis=1, keepdims=True))
        return j - 1, tuple(logs2), tuple(outs2)

    def cond(carry):
        j, logs, _ = carry
        live = logs[0]
        for h in range(1, N_HEADS):
            live = jnp.maximum(live, logs[h])
        return (j >= 0) & (jnp.max(live) >= EXP_ZERO_BELOW)

    init = (i, tuple(jnp.zeros((TQ, 1), F32) for _ in range(N_HEADS)),
            tuple(jnp.zeros((TQ, HEAD_DIM), F32) for _ in range(N_HEADS)))
    _, _, outs = lax.while_loop(cond, body, init)
    o_ref[...] = jnp.concatenate(outs, axis=1).astype(o_ref.dtype)


def _sb_prompt(p, bsz, seq):
    nq = seq // TQ
    q_spec = pl.BlockSpec((TQ, W_HEADS), lambda b, i: (b * nq + i, 0))
    kv_spec = pl.BlockSpec((None, seq, W_HEADS), lambda b, i: (b, 0, 0))
    return pl.pallas_call(
        _sb_prompt_kernel,
        out_shape=jax.ShapeDtypeStruct((bsz * seq, W_HEADS), BF16),
        grid=(bsz, nq),
        in_specs=[q_spec, kv_spec, kv_spec],
        out_specs=q_spec,
        compiler_params=pltpu.CompilerParams(dimension_semantics=("arbitrary", "arbitrary"),
                                             vmem_limit_bytes=VMEM_LIMIT),
        name="sb_prompt",
    )(p["qb"], p["kbb"].reshape(bsz, seq, W_HEADS), p["vbb"].reshape(bsz, seq, W_HEADS))


def _head_sums(x):
    return jnp.sum(x.reshape(N_HEADS, HEAD_DIM, x.shape[-1]), axis=1)


def _head_bcast(x):
    return jnp.broadcast_to(x[:, None, :], (N_HEADS, HEAD_DIM, LANES)).reshape(W_HEADS, LANES)


def _column(x, b):
    lane = lax.broadcasted_iota(I32, x.shape, 1)
    col = jnp.sum(jnp.where(lane == b, x, 0.0), axis=1, keepdims=True)
    return jnp.broadcast_to(col, x.shape)


def _set_column(ref, b, col):
    lane = lax.broadcasted_iota(I32, ref.shape, 1)
    ref[...] = jnp.where(lane == b, jnp.broadcast_to(col, ref.shape), ref[...])


def _page_specs(n_ops, rows, order):
    return [pl.BlockSpec((None, None, rows, PAGE),
                         lambda b, j, pt, u=u: (0, pt[b, order(j, u)], 0, 0))
            for u in range(n_ops)]


def _idx_scores_kernel(pt_ref, qi_ref, w_ref, kinew_ref, *rest):
    pages = rest[:IDX_PAGES_PER_STEP]
    o_ref = rest[IDX_PAGES_PER_STEP]
    j = pl.program_id(1)
    qi = qi_ref[...]
    w = w_ref[...] * (N_HEADS ** -0.5)

    for u in range(IDX_PAGES_PER_STEP):
        s = jnp.dot(qi, pages[u][...].astype(BF16), preferred_element_type=F32)
        o_ref[pl.ds(j * IDX_PAGES_PER_STEP + u, 1), :] = jnp.sum(jnp.maximum(s, 0.0) * w, axis=0, keepdims=True)

    @pl.when(j == pl.num_programs(1) - 1)
    def _():
        s_new = jnp.sum(qi.astype(F32) * kinew_ref[...].astype(F32), axis=1, keepdims=True)
        r_new = jnp.sum(jnp.maximum(s_new, 0.0) * w, axis=0, keepdims=True)
        n_slots = pl.num_programs(1) * IDX_PAGES_PER_STEP
        o_ref[pl.ds(n_slots, 8), :] = jnp.broadcast_to(r_new, (8, LANES))


def _idx_scores(page_table, qi3, w3, kinew3, cache_ki_t):
    dbs, n_pages = page_table.shape
    n_steps = n_pages // IDX_PAGES_PER_STEP
    grid_spec = pltpu.PrefetchScalarGridSpec(
        num_scalar_prefetch=1,
        grid=(dbs, n_steps),
        in_specs=[pl.BlockSpec((None, N_HEADS, IDX_DIM), lambda b, j, pt: (b, 0, 0)),
                  pl.BlockSpec((None, N_HEADS, 1), lambda b, j, pt: (b, 0, 0)),
                  pl.BlockSpec((None, 1, IDX_DIM), lambda b, j, pt: (b, 0, 0))]
        + _page_specs(IDX_PAGES_PER_STEP, IDX_DIM, lambda j, u: j * IDX_PAGES_PER_STEP + u),
        out_specs=pl.BlockSpec((None, n_pages + 8, LANES), lambda b, j, pt: (b, 0, 0)),
    )
    return pl.pallas_call(
        _idx_scores_kernel,
        out_shape=jax.ShapeDtypeStruct((dbs, n_pages + 8, LANES), F32),
        grid_spec=grid_spec,
        compiler_params=pltpu.CompilerParams(dimension_semantics=("arbitrary", "arbitrary")),
        name="dsa_sample_scores",
    )(page_table, qi3, w3, kinew3, *([cache_ki_t] * IDX_PAGES_PER_STEP))


def _sample_select_kernel(s_ref, bias_ref, score_ref, *, n_valid):
    rows, width = s_ref.shape
    col = lax.broadcasted_iota(I32, (rows, CK), 1)
    for c in range(width // CK):
        sl = slice(c * CK, (c + 1) * CK)
        score_ref[:, sl] = _masked_scores(s_ref[:, sl], (col + c * CK) < n_valid)
    _select_topk(score_ref, bias_ref, width // CK, TOPK_MAX)


def _sample_select(scores2d, n_valid):
    rows, width = scores2d.shape
    return pl.pallas_call(
        functools.partial(_sample_select_kernel, n_valid=n_valid),
        out_shape=jax.ShapeDtypeStruct((rows, width), F32),
        grid=(1,),
        in_specs=[pl.BlockSpec((rows, width), lambda i: (0, 0))],
        out_specs=pl.BlockSpec((rows, width), lambda i: (0, 0)),
        scratch_shapes=[pltpu.VMEM((rows, width), F32)],
        compiler_params=pltpu.CompilerParams(dimension_semantics=("arbitrary",),
                                             vmem_limit_bytes=VMEM_LIMIT),
        name="dsa_sample_select",
    )(scores2d)


def _dsa_sample_kernel(pt_ref, qt_ref, knt_ref, vnt_ref, bias_ref, *rest):
    n_ops = DSA_PAGES_PER_STEP
    kpages = rest[:n_ops]
    vpages = rest[n_ops:2 * n_ops]
    o_ref, qc_ref, m_ref, l_ref, acc_ref = rest[2 * n_ops:]
    b = pl.program_id(0)
    j = pl.program_id(1)
    n_steps = pl.num_programs(1)

    @pl.when((b == 0) & (j == 0))
    def _():
        o_ref[...] = jnp.zeros(o_ref.shape, F32)

    @pl.when(j == 0)
    def _():
        qc_ref[...] = _column(qt_ref[...], b)
        m_ref[...] = jnp.full(m_ref.shape, NEG, F32)
        l_ref[...] = jnp.zeros(l_ref.shape, F32)
        acc_ref[...] = jnp.zeros(acc_ref.shape, F32)

    def update(k, v, bias_row):
        lg = _head_sums(k * qc_ref[...]) + bias_row
        m = m_ref[...]
        m_new = jnp.maximum(m, jnp.max(lg, axis=1, keepdims=True))
        alpha = jnp.exp(m - m_new)
        p = jnp.exp(lg - m_new)
        l_ref[...] = alpha * l_ref[...] + jnp.sum(p, axis=1, keepdims=True)
        acc_ref[...] = _head_bcast(alpha) * acc_ref[...] + _head_bcast(p) * v
        m_ref[...] = m_new

    for u in range(n_ops):
        update(kpages[u][...], vpages[u][...], bias_ref[pl.ds(j * n_ops + u, 1), :])

    @pl.when(j == n_steps - 1)
    def _():
        lane = lax.broadcasted_iota(I32, (1, LANES), 1)
        bias_row = jnp.where(lane == 0, bias_ref[pl.ds(n_steps * n_ops, 1), :], NEG)
        update(_column(knt_ref[...], b), _column(vnt_ref[...], b), bias_row)
        o = acc_ref[...] * _head_bcast(1.0 / l_ref[...])
        _set_column(o_ref, b, jnp.sum(o, axis=1, keepdims=True))


def _dsa_sample(page_table, qt, knt, vnt, bias3, cache_k, cache_v):
    dbs, n_pages = page_table.shape
    n_ops = DSA_PAGES_PER_STEP
    n_steps = n_pages // n_ops
    order = lambda j, u: j * n_ops + u
    full = pl.BlockSpec((W_HEADS, LANES), lambda b, j, pt: (0, 0))
    grid_spec = pltpu.PrefetchScalarGridSpec(
        num_scalar_prefetch=1,
        grid=(dbs, n_steps),
        in_specs=[full, full, full,
                  pl.BlockSpec((None, n_pages + 8, LANES), lambda b, j, pt: (b, 0, 0))]
        + _page_specs(n_ops, W_HEADS, order) + _page_specs(n_ops, W_HEADS, order),
        out_specs=full,
        scratch_shapes=[pltpu.VMEM((W_HEADS, LANES), F32), pltpu.VMEM((N_HEADS, 1), F32),
                        pltpu.VMEM((N_HEADS, 1), F32), pltpu.VMEM((W_HEADS, LANES), F32)],
    )
    return pl.pallas_call(
        _dsa_sample_kernel,
        out_shape=jax.ShapeDtypeStruct((W_HEADS, dbs), F32),
        grid_spec=grid_spec,
        compiler_params=pltpu.CompilerParams(dimension_semantics=("arbitrary", "arbitrary"),
                                             vmem_limit_bytes=VMEM_LIMIT),
        name="dsa_sample",
    )(page_table, qt, knt, vnt, bias3, *([cache_k] * n_ops), *([cache_v] * n_ops))


def _sb_sample_kernel(pt_ref, qt_ref, kc_hbm, vc_hbm, o_ref, kbuf, vbuf, sem, qc_ref, out_ref):
    b = pl.program_id(0)
    n_seq = pl.num_programs(0)
    last = pt_ref.shape[1] - 1
    after = _after_matrix(PAGE)

    def copies(seq, page, slot):
        phys = pt_ref[seq, page]
        return (pltpu.make_async_copy(kc_hbm.at[phys], kbuf.at[slot], sem.at[0, slot]),
                pltpu.make_async_copy(vc_hbm.at[phys], vbuf.at[slot], sem.at[1, slot]))

    def start(seq, page, slot):
        for c in copies(seq, page, slot):
            c.start()

    def wait(seq, page, slot):
        for c in copies(seq, page, slot):
            c.wait()

    first_slot = 2 + b % 2

    def slot_of(page):
        return jnp.where(page == last, first_slot, page % 2)

    @pl.when(b == 0)
    def _():
        start(0, last, 2)
        o_ref[...] = jnp.zeros(o_ref.shape, F32)

    @pl.when(b + 1 < n_seq)
    def _():
        start(b + 1, last, 2 + (b + 1) % 2)

    qc_ref[...] = _column(qt_ref[...], b)
    out_ref[...] = jnp.zeros(out_ref.shape, F32)

    def body(carry):
        page, acc_log = carry
        slot = slot_of(page)

        @pl.when(page > 0)
        def _():
            start(b, page - 1, (page - 1) % 2)

        wait(b, page, slot)
        z = _head_sums(kbuf[slot] * qc_ref[...])
        ln = _log_sigmoid_neg(z)
        between = _suffix_sum_excl(ln, after) + acc_log
        a = jnp.exp(z + ln + between)
        out_ref[...] += _head_bcast(a) * vbuf[slot]
        return page - 1, acc_log + jnp.sum(ln, axis=1, keepdims=True)

    def cond(carry):
        page, acc_log = carry
        return (page >= 0) & (jnp.max(acc_log) >= EXP_ZERO_BELOW)

    page_end, _ = lax.while_loop(cond, body, (jnp.int32(last), jnp.zeros((N_HEADS, 1), F32)))

    @pl.when(page_end >= 0)
    def _():
        wait(b, page_end, slot_of(page_end))

    _set_column(o_ref, b, jnp.sum(out_ref[...], axis=1, keepdims=True))


def _sb_sample(page_table, qt, cache_k, cache_v):
    dbs, n_pages = page_table.shape
    full = pl.BlockSpec((W_HEADS, LANES), lambda b, pt: (0, 0))
    grid_spec = pltpu.PrefetchScalarGridSpec(
        num_scalar_prefetch=1,
        grid=(dbs,),
        in_specs=[full, pl.BlockSpec(memory_space=pl.ANY), pl.BlockSpec(memory_space=pl.ANY)],
        out_specs=full,
        scratch_shapes=[pltpu.VMEM((4, W_HEADS, PAGE), F32), pltpu.VMEM((4, W_HEADS, PAGE), F32),
                        pltpu.SemaphoreType.DMA((2, 4)),
                        pltpu.VMEM((W_HEADS, LANES), F32), pltpu.VMEM((W_HEADS, LANES), F32)],
    )
    return pl.pallas_call(
        _sb_sample_kernel,
        out_shape=jax.ShapeDtypeStruct((W_HEADS, dbs), F32),
        grid_spec=grid_spec,
        compiler_params=pltpu.CompilerParams(dimension_semantics=("arbitrary",)),
        name="sb_sample",
    )(page_table, qt, cache_k, cache_v)


def _mix_out_kernel(x_ref, oa_ref, ob_ref, ga_ref, gb_ref, gate_ref, woa_ref, wob_ref, wout_ref, g_ref, o_ref):
    ma = jnp.dot(oa_ref[...], woa_ref[...], preferred_element_type=F32)
    mb = jnp.dot(ob_ref[...], wob_ref[...], preferred_element_type=F32)
    merged = ga_ref[...].astype(F32) * ma + gb_ref[...].astype(F32) * mb
    y = jnp.dot(merged.astype(BF16), wout_ref[...], preferred_element_type=F32)
    o_ref[...] = x_ref[...] + gate_ref[0] * (_rms(y) * g_ref[...])


def _mix_out(x2d, oa, ob, ga, gb, mod3, woa, wob, wout, g_post, tm, rows_per_mod):
    rows = x2d.shape[0]
    r_mod = mod3.shape[1]
    tiles_per_mod = rows_per_mod // tm

    def row_spec(width):
        return pl.BlockSpec((tm, width), lambda i: (i, 0))

    def full(shape):
        return pl.BlockSpec(shape, lambda i: (0,) * len(shape))

    return pl.pallas_call(
        _mix_out_kernel,
        out_shape=jax.ShapeDtypeStruct((rows, D_MODEL), F32),
        grid=(rows // tm,),
        in_specs=[row_spec(D_MODEL), row_spec(W_HEADS), row_spec(W_HEADS), row_spec(D_MODEL), row_spec(D_MODEL),
                  pl.BlockSpec((1, r_mod, D_MODEL), lambda i: (i // tiles_per_mod, 0, 2)),
                  full((W_HEADS, D_MODEL)), full((W_HEADS, D_MODEL)), full((D_MODEL, D_MODEL)),
                  full((1, D_MODEL))],
        out_specs=row_spec(D_MODEL),
        compiler_params=pltpu.CompilerParams(dimension_semantics=("arbitrary",),
                                             vmem_limit_bytes=VMEM_LIMIT),
        name="mix_out",
    )(x2d, oa, ob, ga, gb, mod3, woa, wob, wout, g_post)


def _ffn_kernel(x_ref, shift_ref, scale_ref, gate_ref, gpre_ref, gpost_ref, win_ref, wout_ref, o_ref, *, d_ff, n_split):
    x = x_ref[...]
    h = (_rms(x) * gpre_ref[...] * (1.0 + scale_ref[0]) + shift_ref[0]).astype(BF16)
    part = d_ff // n_split
    y = jnp.zeros(x.shape, F32)
    for s in range(n_split):
        g = jnp.dot(h, win_ref[:, s * part:(s + 1) * part], preferred_element_type=F32)
        u = jnp.dot(h, win_ref[:, d_ff + s * part:d_ff + (s + 1) * part], preferred_element_type=F32)
        act = (g * (1.0 / (1.0 + jnp.exp(-g))) * u).astype(BF16)
        y = y + jnp.dot(act, wout_ref[s * part:(s + 1) * part, :], preferred_element_type=F32)
    o_ref[...] = x + gate_ref[0] * (_rms(y) * gpost_ref[...])


def _ffn(x2d, mod3, g_pre, g_post, w_in, w_out, tm, rows_per_mod):
    rows = x2d.shape[0]
    r_mod = mod3.shape[1]
    tiles_per_mod = rows_per_mod // tm
    d_ff = w_out.shape[0]
    n_split = 2
    assert d_ff % (n_split * LANES) == 0

    def mod_spec(chunk):
        return pl.BlockSpec((1, r_mod, D_MODEL), lambda i: (i // tiles_per_mod, 0, chunk))

    def full(shape):
        return pl.BlockSpec(shape, lambda i: (0,) * len(shape))

    return pl.pallas_call(
        functools.partial(_ffn_kernel, d_ff=d_ff, n_split=n_split),
        out_shape=jax.ShapeDtypeStruct((rows, D_MODEL), F32),
        grid=(rows // tm,),
        in_specs=[pl.BlockSpec((tm, D_MODEL), lambda i: (i, 0)), mod_spec(3), mod_spec(4), mod_spec(5),
                  full((1, D_MODEL)), full((1, D_MODEL)), full((D_MODEL, 2 * d_ff)), full((d_ff, D_MODEL))],
        out_specs=pl.BlockSpec((tm, D_MODEL), lambda i: (i, 0)),
        compiler_params=pltpu.CompilerParams(dimension_semantics=("arbitrary",),
                                             vmem_limit_bytes=VMEM_LIMIT),
        name="ffn",
    )(x2d, mod3, mod3, mod3, g_pre, g_post, w_in, w_out)


def _pack_w_in(w):
    a = W_HEADS
    qa, ka, va, qi = w[:, 0:a], w[:, a:2 * a], w[:, 2 * a:3 * a], w[:, 3 * a:4 * a]
    o = 4 * a
    ki, wi = w[:, o:o + IDX_DIM], w[:, o + IDX_DIM:o + IDX_DIM + N_HEADS]
    o += IDX_DIM + N_HEADS
    qb, kb, vb = w[:, o:o + a], w[:, o + a:o + 2 * a], w[:, o + 2 * a:o + 3 * a]
    o += 3 * a
    ga, gb = w[:, o:o + D_MODEL], w[:, o + D_MODEL:o + 2 * D_MODEL]
    pad = jnp.zeros((w.shape[0], LANES - IDX_DIM - N_HEADS), w.dtype)
    return jnp.concatenate([qa, ka, va, qi, ki, wi, pad, qb, kb, vb, ga, gb], axis=1).astype(BF16)


def _rope_tables(pos):
    half = HEAD_DIM // 2
    freqs = ROPE_THETA ** (-jnp.arange(half, dtype=F32) / half)
    ang = pos.astype(F32)[:, None] * freqs[None, :]
    cos, sin = jnp.cos(ang), jnp.sin(ang)
    cos_h = jnp.concatenate([cos, cos], axis=1)
    sin_h = jnp.concatenate([-sin, sin], axis=1)
    cos2 = jnp.concatenate([cos_h, cos_h], axis=1)
    sin2 = jnp.concatenate([sin_h, sin_h], axis=1)
    cosk = jnp.concatenate([cos_h, jnp.ones_like(cos_h)], axis=1)
    sink = jnp.concatenate([sin_h, jnp.zeros_like(sin_h)], axis=1)
    return cos2, sin2, cosk, sink


def _token_minor_pages(cache):
    depth, n_pool = cache.shape[:2]
    return jnp.transpose(cache, (0, 1, 3, 4, 2)).reshape(depth, n_pool, W_HEADS, PAGE)


def kernel(x_prompt, x_sample, c_prompt, c_sample, cache_k_a, cache_v_a, cache_k_idx, cache_k_b, cache_v_b,
           page_table, w_ada, b_ada, g_pre_mix, g_post_mix, w_in, w_o_a, w_o_b, w_out, g_pre_ffn, g_post_ffn,
           w_ffn_in, w_ffn_out):
    depth = w_ada.shape[0]
    assert depth == 1
    bsz, seq, _ = x_prompt.shape
    dbs, dec_seq, _ = x_sample.shape
    assert dec_seq == 1 and dbs == LANES
    n_pages = page_table.shape[1]
    past = n_pages * PAGE
    assert past + dec_seq > TOPK_MAX
    layer = 0

    n_c = bsz + dbs
    c_all = jnp.concatenate([c_prompt, c_sample, jnp.zeros((-n_c % 8, D_MODEL), F32)], axis=0)
    mod = _modulation(c_all, w_ada[layer], b_ada[layer][None, :])
    mod_p = mod[:bsz].reshape(bsz, 1, 6 * D_MODEL)
    mod_s = mod[bsz:n_c].reshape(1, dbs, 6 * D_MODEL)

    w_packed = _pack_w_in(w_in[layer])
    woa, wob, wout = w_o_a[layer].astype(BF16), w_o_b[layer].astype(BF16), w_out[layer].astype(BF16)
    wfi, wfo = w_ffn_in[layer].astype(BF16), w_ffn_out[layer].astype(BF16)
    g1, g2 = g_pre_mix[layer][None, :], g_post_mix[layer][None, :]
    g3, g4 = g_pre_ffn[layer][None, :], g_post_ffn[layer][None, :]

    tm = 256
    xp = x_prompt.reshape(bsz * seq, D_MODEL)
    tabs_p = _rope_tables(jnp.arange(seq, dtype=I32))
    pp = _projection(xp, g1, mod_p, w_packed, tabs_p, tm, seq, seq // tm)
    oa_p = _dsa_prompt(pp, bsz, seq)
    ob_p = _sb_prompt(pp, bsz, seq)
    x1_p = _mix_out(xp, oa_p, ob_p, pp["ga"], pp["gb"], mod_p, woa, wob, wout, g2, tm, seq)
    y_p = _ffn(x1_p, mod_p, g3, g4, wfi, wfo, tm, seq).reshape(bsz, seq, D_MODEL)

    xs = x_sample.reshape(dbs, D_MODEL)
    tabs_s = _rope_tables(jnp.full((dbs,), past, dtype=I32))
    ps = _projection(xs, g1, mod_s, w_packed, tabs_s, dbs, dbs, 1)

    qi3 = ps["qi"].reshape(dbs, N_HEADS, IDX_DIM)
    w3 = ps["kiwf"][:, IDX_DIM:IDX_DIM + N_HEADS].reshape(dbs, N_HEADS, 1)
    kinew3 = ps["kib"][:, :IDX_DIM].reshape(dbs, 1, IDX_DIM)
    cki = jnp.transpose(cache_k_idx, (0, 1, 3, 2))
    scores = _idx_scores(page_table, qi3, w3, kinew3, cki)
    bias = _sample_select(scores.reshape(dbs, (n_pages + 8) * LANES), past + 1)
    bias3 = bias.reshape(dbs, n_pages + 8, LANES)

    qat = ps["qa"].astype(F32).T
    oa_s = _dsa_sample(page_table, qat, ps["kat"][0], ps["vat"][0], bias3,
                       _token_minor_pages(cache_k_a), _token_minor_pages(cache_v_a)).T
    qbt = ps["qb"].astype(F32).T
    ob_s = _sb_sample(page_table, qbt, _token_minor_pages(cache_k_b)[layer],
                      _token_minor_pages(cache_v_b)[layer]).T

    x1_s = _mix_out(xs, oa_s.astype(BF16), ob_s.astype(BF16), ps["ga"], ps["gb"], mod_s,
                    woa, wob, wout, g2, dbs, dbs)
    y_s = _ffn(x1_s, mod_s, g3, g4, wfi, wfo, dbs, dbs).reshape(dbs, 1, D_MODEL)

    def rows_p(t, width):
        if width == IDX_DIM:
            return jnp.transpose(t, (0, 2, 1)).reshape(depth, bsz, seq, IDX_DIM)
        return jnp.transpose(t.reshape(bsz, N_HEADS, HEAD_DIM, seq), (0, 3, 1, 2)).reshape(
            depth, bsz, seq, N_HEADS, HEAD_DIM)

    def rows_s(t, width):
        if width == IDX_DIM:
            return jnp.transpose(t[0], (1, 0)).reshape(depth, dbs, dec_seq, IDX_DIM)
        return jnp.transpose(t[0].reshape(N_HEADS, HEAD_DIM, dbs), (2, 0, 1)).reshape(
            depth, dbs, dec_seq, N_HEADS, HEAD_DIM)

    return (y_p, y_s,
            rows_p(pp["kat"], W_HEADS), rows_p(pp["vat"], W_HEADS), rows_p(pp["kit"], IDX_DIM),
            rows_p(pp["kbt"], W_HEADS), rows_p(pp["vbt"], W_HEADS),
            rows_s(ps["kat"], W_HEADS), rows_s(ps["vat"], W_HEADS), rows_s(ps["kit"], IDX_DIM),
            rows_s(ps["kbt"], W_HEADS), rows_s(ps["vbt"], W_HEADS))
```

```python
import functools

import jax
import jax.numpy as jnp
from jax import lax
from jax.experimental import pallas as pl
from jax.experimental.pallas import tpu as pltpu

F32 = jnp.float32
BF16 = jnp.bfloat16
I32 = jnp.int32

D_MODEL = 1024
HEAD_DIM = 64
N_HEADS = 8
W_HEADS = N_HEADS * HEAD_DIM
IDX_DIM = 64
TOPK_MAX = 256
ROPE_THETA = 10000.0
RMS_EPS = 1e-6
PAGE = 128

LANES = 128
INT_MIN = -(2**31)
NEG = -1e30
EXP_ZERO_BELOW = -104.0
VMEM_LIMIT = 56 * 1024 * 1024

TQ = 256
TK_SB = 128
CK = 512
SLAB = 128
HEAD_GROUP = 4
SB_HEAD_GROUP = 8
DSA_PAGES_PER_STEP = 8
IDX_PAGES_PER_STEP = 16

_SEC = {}
_off = 0
for _name, _w in (("qa", 512), ("ka", 512), ("va", 512), ("vas", N_HEADS * SLAB), ("qi", 512), ("kiw", 128),
                  ("qb", 512), ("kb", 512), ("vb", 512), ("ga", 1024), ("gb", 1024)):
    _SEC[_name] = (_off, _off + _w)
    _off += _w
W_IN_PACKED = _off


def _nt_dot(a, b):
    return lax.dot_general(a, b, (((1,), (1,)), ((), ())), preferred_element_type=F32)


def _rms(x):
    return x * lax.rsqrt(jnp.mean(x * x, axis=-1, keepdims=True) + RMS_EPS)


def _mod_kernel(c_ref, w_ref, b_ref, o_ref):
    c = c_ref[...]
    s = c * (1.0 / (1.0 + jnp.exp(-c)))
    o_ref[...] = jnp.dot(s, w_ref[...], preferred_element_type=F32,
                         precision=lax.Precision.HIGHEST) + b_ref[...]


def _modulation(c_all, w_ada, b_ada):
    m = c_all.shape[0]
    n = w_ada.shape[1]
    tn = 512
    return pl.pallas_call(
        _mod_kernel,
        out_shape=jax.ShapeDtypeStruct((m, n), F32),
        grid=(n // tn,),
        in_specs=[pl.BlockSpec((m, D_MODEL), lambda j: (0, 0)),
                  pl.BlockSpec((D_MODEL, tn), lambda j: (0, j)),
                  pl.BlockSpec((1, tn), lambda j: (0, j))],
        out_specs=pl.BlockSpec((m, tn), lambda j: (0, j)),
        compiler_params=pltpu.CompilerParams(dimension_semantics=("arbitrary",)),
        name="adaln_mod",
    )(c_all, w_ada, b_ada)


def _proj_kernel(x_ref, g_ref, shift_ref, scale_ref, w_ref, cos_ref, sin_ref, cosk_ref, sink_ref,
                 qa_ref, vas_ref, qi_ref, kiwf_ref, kib_ref, qb_ref, vbb_ref, ga_ref, gb_ref,
                 kat_ref, vat_ref, kit_ref, kbt_ref, vbt_ref, katb_ref, kitb_ref, kbtb_ref):
    tm = x_ref.shape[0]
    h = (_rms(x_ref[...]) * g_ref[...] * (1.0 + scale_ref[0]) + shift_ref[0]).astype(BF16)

    def mm(name):
        a, b = _SEC[name]
        return jnp.dot(h, w_ref[:, a:b], preferred_element_type=F32)

    lane = lax.broadcasted_iota(I32, (tm, LANES), 1)
    first_half = (lane & 32) == 0

    def rope(p, cos, sin):
        outs = []
        for c in range(p.shape[1] // LANES):
            xc = p[:, c * LANES:(c + 1) * LANES]
            partner = jnp.where(first_half, pltpu.roll(xc, LANES - 32, 1), pltpu.roll(xc, 32, 1))
            outs.append(xc * cos + partner * sin)
        return outs[0] if len(outs) == 1 else jnp.concatenate(outs, axis=1)

    cos = cos_ref[...]
    sin = sin_ref[...]
    scale = HEAD_DIM ** -0.5
    qa_ref[...] = (rope(mm("qa"), cos, sin) * scale).astype(BF16)
    kat = rope(mm("ka"), cos, sin).T
    kat_ref[...] = kat
    katb_ref[...] = kat.astype(BF16)
    vat_ref[...] = mm("va").T
    vas = mm("vas")
    slab_lane = lax.broadcasted_iota(I32, vas.shape, 1) & (SLAB - 1)
    vas_ref[...] = jnp.where(slab_lane == HEAD_DIM, 1.0, vas).astype(BF16)
    qi_ref[...] = (rope(mm("qi"), cos, sin) * (IDX_DIM ** -0.5)).astype(BF16)
    kiw = rope(mm("kiw"), cosk_ref[...], sink_ref[...])
    kiwf_ref[...] = kiw
    kib_ref[...] = kiw.astype(BF16)
    kit = kiw.T[:IDX_DIM, :]
    kit_ref[...] = kit
    kitb_ref[...] = kit.astype(BF16)
    qb_ref[...] = (mm("qb") * scale).astype(BF16)
    kbt = mm("kb").T
    kbt_ref[...] = kbt
    kbtb_ref[...] = kbt.astype(BF16)
    vb = mm("vb")
    vbb_ref[...] = vb.astype(BF16)
    vbt_ref[...] = vb.T
    ga = mm("ga")
    ga_ref[...] = (1.0 / (1.0 + jnp.exp(-ga))).astype(BF16)
    gb = mm("gb")
    gb_ref[...] = (1.0 / (1.0 + jnp.exp(-gb))).astype(BF16)


def _projection(x2d, g_pre, mod3, w_packed, tabs, tm, rows_per_mod, n_pos_tiles):
    rows = x2d.shape[0]
    n_tiles = rows // tm
    r_mod = mod3.shape[1]
    tiles_per_mod = rows_per_mod // tm
    groups = rows // rows_per_mod

    def row_spec(width):
        return pl.BlockSpec((tm, width), lambda i: (i, 0))

    def t_spec(width):
        return pl.BlockSpec((None, width, tm), lambda i: (i // tiles_per_mod, 0, i % tiles_per_mod))

    def mod_spec(chunk):
        return pl.BlockSpec((1, r_mod, D_MODEL), lambda i: (i // tiles_per_mod, 0, chunk))

    tab_spec = pl.BlockSpec((tm, LANES), lambda i: (i % n_pos_tiles, 0))
    outs = [("qa", 512, BF16), ("vas", N_HEADS * SLAB, BF16),
            ("qi", 512, BF16), ("kiwf", 128, F32), ("kib", 128, BF16), ("qb", 512, BF16),
            ("vbb", 512, BF16), ("ga", 1024, BF16), ("gb", 1024, BF16)]
    outs_t = [("kat", W_HEADS, F32), ("vat", W_HEADS, F32), ("kit", IDX_DIM, F32), ("kbt", W_HEADS, F32),
              ("vbt", W_HEADS, F32), ("katb", W_HEADS, BF16), ("kitb", IDX_DIM, BF16), ("kbtb", W_HEADS, BF16)]
    res = pl.pallas_call(
        _proj_kernel,
        out_shape=[jax.ShapeDtypeStruct((rows, w), dt) for _, w, dt in outs]
        + [jax.ShapeDtypeStruct((groups, w, rows_per_mod), dt) for _, w, dt in outs_t],
        grid=(n_tiles,),
        in_specs=[row_spec(D_MODEL),
                  pl.BlockSpec((1, D_MODEL), lambda i: (0, 0)),
                  mod_spec(0), mod_spec(1),
                  pl.BlockSpec((D_MODEL, W_IN_PACKED), lambda i: (0, 0)),
                  tab_spec, tab_spec, tab_spec, tab_spec],
        out_specs=[row_spec(w) for _, w, _ in outs] + [t_spec(w) for _, w, _ in outs_t],
        compiler_params=pltpu.CompilerParams(dimension_semantics=("arbitrary",),
                                             vmem_limit_bytes=VMEM_LIMIT),
        name="in_proj",
    )(x2d, g_pre, mod3, mod3, w_packed, *tabs)
    return {n: r for n, r in zip([o[0] for o in outs] + [o[0] for o in outs_t], res)}


def _masked_scores(acc, valid):
    return jnp.where(valid, acc, -jnp.inf)


def _key_to_float(key):
    bits = jnp.where(key < 0, (key - 1) ^ jnp.int32(0x7FFFFFFF), key)
    return pltpu.bitcast(bits, F32)


def _lane_fold(x):
    out = x[:, :LANES]
    for q in range(1, x.shape[1] // LANES):
        out = out + x[:, q * LANES:(q + 1) * LANES]
    return out


def _select_topk(score_ref, bias_ref, n_chunks, k):
    rows = score_ref.shape[0]
    kf = float(k)

    def chunk_slice(c):
        return pl.ds(pl.multiple_of(c * CK, CK), CK)

    def count_ge(cand):
        def chunk(c, cnt):
            return cnt + _lane_fold(jnp.where(score_ref[:, chunk_slice(c)] >= cand, 1.0, 0.0))
        cnt = lax.fori_loop(0, n_chunks, chunk, jnp.zeros((rows, LANES), F32))
        return jnp.sum(cnt, axis=1, keepdims=True)

    def bit_step(it, tu):
        cand_u = tu | jnp.left_shift(jnp.int32(1), 31 - it)
        cnt = count_ge(_key_to_float(cand_u ^ jnp.int32(INT_MIN)))
        return jnp.where(cnt >= kf, cand_u, tu)

    tu = lax.fori_loop(0, 32, bit_step, jnp.zeros((rows, 1), I32))
    ts = jnp.where(tu == 0, -jnp.inf, _key_to_float(tu ^ jnp.int32(INT_MIN)))

    def tally(c, carry):
        cge, cgt = carry
        sc = score_ref[:, chunk_slice(c)]
        ge = sc >= ts
        bias_ref[:, chunk_slice(c)] = jnp.where(ge & (sc > -jnp.inf), 0.0, NEG)
        return (cge + _lane_fold(jnp.where(ge, 1.0, 0.0)), cgt + _lane_fold(jnp.where(sc > ts, 1.0, 0.0)))

    zero = jnp.zeros((rows, LANES), F32)
    cge, cgt = lax.fori_loop(0, n_chunks, tally, (zero, zero))
    cnt_ge = jnp.sum(cge, axis=1, keepdims=True)
    cnt_gt = jnp.sum(cgt, axis=1, keepdims=True)
    tie = (cnt_ge > kf) & (ts > -jnp.inf)

    @pl.when(jnp.max(jnp.where(tie, 1.0, 0.0)) > 0.0)
    def _():
        need = kf - cnt_gt
        r = lax.broadcasted_iota(I32, (LANES, LANES), 0)
        c_ = lax.broadcasted_iota(I32, (LANES, LANES), 1)
        before = jnp.where(r < c_, 1.0, 0.0).astype(BF16)

        def chunk(c, run):
            sl = pl.ds(pl.multiple_of(c * LANES, LANES), LANES)
            sc = score_ref[:, sl]
            eq = jnp.where(sc == ts, 1.0, 0.0)
            rank = jnp.dot(eq.astype(BF16), before, preferred_element_type=F32) + run
            take = (sc > ts) | ((sc == ts) & (rank < need))
            bias_ref[:, sl] = jnp.where(tie, jnp.where(take, 0.0, NEG), bias_ref[:, sl])
            return run + jnp.sum(eq, axis=1, keepdims=True)

        lax.fori_loop(0, n_chunks * (CK // LANES), chunk, jnp.zeros((rows, 1), F32))


def _dsa_prompt_kernel(qi_ref, kiwq_ref, kit_ref, qa_ref, kat_ref, vas_ref, o_ref, score_ref, bias_ref):
    i = pl.program_id(1)
    n_chunks = i // (CK // TQ) + 1
    qi = qi_ref[...]
    w = kiwq_ref[:, IDX_DIM:IDX_DIM + N_HEADS] * (N_HEADS ** -0.5)
    qih = [qi[:, h * IDX_DIM:(h + 1) * IDX_DIM] for h in range(N_HEADS)]
    wh = [w[:, h:h + 1] for h in range(N_HEADS)]
    t_idx = i * TQ + lax.broadcasted_iota(I32, (TQ, CK), 0)
    col = lax.broadcasted_iota(I32, (TQ, CK), 1)

    def score_chunk(c, _):
        start = pl.multiple_of(c * CK, CK)
        kic = kit_ref[:, pl.ds(start, CK)]
        acc = jnp.zeros((TQ, CK), F32)
        for h in range(N_HEADS):
            s = jnp.dot(qih[h], kic, preferred_element_type=F32)
            acc = acc + jnp.maximum(s, 0.0) * wh[h]
        score_ref[:, pl.ds(start, CK)] = _masked_scores(acc, (col + start) <= t_idx)
        return 0

    lax.fori_loop(0, n_chunks, score_chunk, 0)
    _select_topk(score_ref, bias_ref, n_chunks, TOPK_MAX)

    qa = qa_ref[...]
    qah = [qa[:, h * HEAD_DIM:(h + 1) * HEAD_DIM] for h in range(N_HEADS)]

    for h0 in range(0, N_HEADS, HEAD_GROUP):
        heads = range(h0, h0 + HEAD_GROUP)

        def att_chunk(c, carry, heads=heads):
            ms, accs = carry
            start = pl.multiple_of(c * CK, CK)
            bias = bias_ref[:, pl.ds(start, CK)]
            ms2, accs2 = [], []
            for n, h in enumerate(heads):
                kc = kat_ref[h * HEAD_DIM:(h + 1) * HEAD_DIM, pl.ds(start, CK)]
                lg = jnp.dot(qah[h], kc, preferred_element_type=F32) + bias
                m_new = jnp.maximum(ms[n], jnp.max(lg, axis=1, keepdims=True))
                alpha = jnp.exp(ms[n] - m_new)
                p = jnp.exp(lg - m_new).astype(BF16)
                pv = jnp.dot(p, vas_ref[pl.ds(start, CK), h * SLAB:(h + 1) * SLAB], preferred_element_type=F32)
                ms2.append(m_new)
                accs2.append(alpha * accs[n] + pv)
            return tuple(ms2), tuple(accs2)

        init = (tuple(jnp.full((TQ, 1), NEG, F32) for _ in heads),
                tuple(jnp.zeros((TQ, SLAB), F32) for _ in heads))
        _, accs = lax.fori_loop(0, n_chunks, att_chunk, init)
        for n, h in enumerate(heads):
            o = accs[n][:, :HEAD_DIM] / accs[n][:, HEAD_DIM:HEAD_DIM + 1]
            o_ref[:, h * HEAD_DIM:(h + 1) * HEAD_DIM] = o.astype(o_ref.dtype)


def _dsa_prompt(p, bsz, seq):
    nq = seq // TQ

    def q_spec(width):
        return pl.BlockSpec((TQ, width), lambda b, i: (b * nq + i, 0))

    def batch_spec(rows, width):
        return pl.BlockSpec((None, rows, width), lambda b, i: (b, 0, 0))

    return pl.pallas_call(
        _dsa_prompt_kernel,
        out_shape=jax.ShapeDtypeStruct((bsz * seq, W_HEADS), BF16),
        grid=(bsz, nq),
        in_specs=[q_spec(W_HEADS), q_spec(LANES), batch_spec(IDX_DIM, seq), q_spec(W_HEADS),
                  batch_spec(W_HEADS, seq), batch_spec(seq, N_HEADS * SLAB)],
        out_specs=q_spec(W_HEADS),
        scratch_shapes=[pltpu.VMEM((TQ, seq), F32), pltpu.VMEM((TQ, seq), F32)],
        compiler_params=pltpu.CompilerParams(dimension_semantics=("arbitrary", "arbitrary"),
                                             vmem_limit_bytes=VMEM_LIMIT),
        name="dsa_prompt",
    )(p["qi"], p["kiwf"], p["kitb"], p["qa"], p["katb"], p["vas"].reshape(bsz, seq, N_HEADS * SLAB))


def _log_sigmoid_neg(z):
    return jnp.minimum(-z, 0.0) - jnp.log1p(jnp.exp(-jnp.abs(z)))


def _suffix_sum_excl(lg, after):
    hi = lg.astype(BF16)
    lo = (lg - hi.astype(F32)).astype(BF16)
    return (jnp.dot(hi, after, preferred_element_type=F32)
            + jnp.dot(lo, after, preferred_element_type=F32))


def _after_matrix(n):
    r = lax.broadcasted_iota(I32, (n, n), 0)
    c = lax.broadcasted_iota(I32, (n, n), 1)
    return jnp.where(r > c, 1.0, 0.0).astype(BF16)


def _sb_prompt_kernel(qb_ref, kbt_ref, vb_ref, o_ref):
    i = pl.program_id(1)
    after = _after_matrix(TK_SB)
    t_idx = i * TQ + lax.broadcasted_iota(I32, (TQ, TK_SB), 0)
    col = lax.broadcasted_iota(I32, (TQ, TK_SB), 1)
    qb = qb_ref[...]
    last_block = (i + 1) * (TQ // TK_SB) - 1

    for h0 in range(0, N_HEADS, SB_HEAD_GROUP):
        heads = range(h0, h0 + SB_HEAD_GROUP)
        qbh = [qb[:, h * HEAD_DIM:(h + 1) * HEAD_DIM] for h in heads]

        def body(carry, heads=heads, qbh=qbh):
            j, logs, outs = carry
            start = pl.multiple_of(j * TK_SB, TK_SB)
            strict = (col + start) < t_idx
            logs2, outs2 = [], []
            for n, h in enumerate(heads):
                hs = slice(h * HEAD_DIM, (h + 1) * HEAD_DIM)
                z = jnp.dot(qbh[n], kbt_ref[hs, pl.ds(start, TK_SB)], preferred_element_type=F32)
                lraw = _log_sigmoid_neg(z)
                lg = jnp.where(strict, lraw, 0.0)
                between = _suffix_sum_excl(lg, after) + logs[n]
                a = jnp.where(strict, jnp.exp(z + lraw + between), 0.0)
                outs2.append(outs[n] + jnp.dot(a.astype(BF16), vb_ref[pl.ds(start, TK_SB), hs],
                                               preferred_element_type=F32))
                logs2.append(logs[n] + jnp.sum(lg, axis=1, keepdims=True))
            return j - 1, tuple(logs2), tuple(outs2)

        def cond(carry):
            j, logs, _ = carry
            live = logs[0]
            for lg in logs[1:]:
                live = jnp.maximum(live, lg)
            return (j >= 0) & (jnp.max(live) >= EXP_ZERO_BELOW)

        init = (last_block, tuple(jnp.zeros((TQ, 1), F32) for _ in heads),
                tuple(jnp.zeros((TQ, HEAD_DIM), F32) for _ in heads))
        _, _, outs = lax.while_loop(cond, body, init)
        for n, h in enumerate(heads):
            o_ref[:, h * HEAD_DIM:(h + 1) * HEAD_DIM] = outs[n].astype(o_ref.dtype)


def _sb_prompt(p, bsz, seq):
    nq = seq // TQ
    q_spec = pl.BlockSpec((TQ, W_HEADS), lambda b, i: (b * nq + i, 0))
    return pl.pallas_call(
        _sb_prompt_kernel,
        out_shape=jax.ShapeDtypeStruct((bsz * seq, W_HEADS), BF16),
        grid=(bsz, nq),
        in_specs=[q_spec, pl.BlockSpec((None, W_HEADS, seq), lambda b, i: (b, 0, 0)),
                  pl.BlockSpec((None, seq, W_HEADS), lambda b, i: (b, 0, 0))],
        out_specs=q_spec,
        compiler_params=pltpu.CompilerParams(dimension_semantics=("arbitrary", "arbitrary"),
                                             vmem_limit_bytes=VMEM_LIMIT),
        name="sb_prompt",
    )(p["qb"], p["kbtb"], p["vbb"].reshape(bsz, seq, W_HEADS))


def _head_sums(x):
    return jnp.sum(x.reshape(N_HEADS, HEAD_DIM, x.shape[-1]), axis=1)


def _head_bcast(x):
    return jnp.broadcast_to(x[:, None, :], (N_HEADS, HEAD_DIM, LANES)).reshape(W_HEADS, LANES)


def _column(x, b):
    lane = lax.broadcasted_iota(I32, x.shape, 1)
    col = jnp.sum(jnp.where(lane == b, x, 0.0), axis=1, keepdims=True)
    return jnp.broadcast_to(col, x.shape)


def _set_column(ref, b, col):
    lane = lax.broadcasted_iota(I32, ref.shape, 1)
    ref[...] = jnp.where(lane == b, jnp.broadcast_to(col, ref.shape), ref[...])


def _page_specs(n_ops, rows, order):
    return [pl.BlockSpec((None, None, rows, PAGE),
                         lambda b, j, pt, u=u: (0, pt[b, order(j, u)], 0, 0))
            for u in range(n_ops)]


def _idx_scores_kernel(pt_ref, qi_ref, w_ref, kinew_ref, *rest):
    pages = rest[:IDX_PAGES_PER_STEP]
    o_ref = rest[IDX_PAGES_PER_STEP]
    j = pl.program_id(1)
    qi = qi_ref[...]
    w = w_ref[...] * (N_HEADS ** -0.5)

    keys = jnp.concatenate([pg[...].astype(BF16) for pg in pages], axis=1)
    s = jnp.dot(qi, keys, preferred_element_type=F32)
    r = jnp.sum(jnp.maximum(s, 0.0) * w, axis=0, keepdims=True)
    for u in range(IDX_PAGES_PER_STEP):
        o_ref[pl.ds(j * IDX_PAGES_PER_STEP + u, 1), :] = r[:, u * PAGE:(u + 1) * PAGE]

    @pl.when(j == pl.num_programs(1) - 1)
    def _():
        s_new = jnp.sum(qi.astype(F32) * kinew_ref[...].astype(F32), axis=1, keepdims=True)
        r_new = jnp.sum(jnp.maximum(s_new, 0.0) * w, axis=0, keepdims=True)
        n_slots = pl.num_programs(1) * IDX_PAGES_PER_STEP
        o_ref[pl.ds(n_slots, 8), :] = jnp.broadcast_to(r_new, (8, LANES))


def _idx_scores(page_table, qi3, w3, kinew3, cache_ki_t):
    dbs, n_pages = page_table.shape
    n_steps = n_pages // IDX_PAGES_PER_STEP
    grid_spec = pltpu.PrefetchScalarGridSpec(
        num_scalar_prefetch=1,
        grid=(dbs, n_steps),
        in_specs=[pl.BlockSpec((None, N_HEADS, IDX_DIM), lambda b, j, pt: (b, 0, 0)),
                  pl.BlockSpec((None, N_HEADS, 1), lambda b, j, pt: (b, 0, 0)),
                  pl.BlockSpec((None, 1, IDX_DIM), lambda b, j, pt: (b, 0, 0))]
        + _page_specs(IDX_PAGES_PER_STEP, IDX_DIM, lambda j, u: j * IDX_PAGES_PER_STEP + u),
        out_specs=pl.BlockSpec((None, n_pages + 8, LANES), lambda b, j, pt: (b, 0, 0)),
    )
    return pl.pallas_call(
        _idx_scores_kernel,
        out_shape=jax.ShapeDtypeStruct((dbs, n_pages + 8, LANES), F32),
        grid_spec=grid_spec,
        compiler_params=pltpu.CompilerParams(dimension_semantics=("arbitrary", "arbitrary")),
        name="dsa_sample_scores",
    )(page_table, qi3, w3, kinew3, *([cache_ki_t] * IDX_PAGES_PER_STEP))


def _sample_select_kernel(s_ref, bias_ref, score_ref, *, n_valid):
    rows, width = s_ref.shape
    col = lax.broadcasted_iota(I32, (rows, CK), 1)
    for c in range(width // CK):
        sl = slice(c * CK, (c + 1) * CK)
        score_ref[:, sl] = _masked_scores(s_ref[:, sl], (col + c * CK) < n_valid)
    _select_topk(score_ref, bias_ref, width // CK, TOPK_MAX)


def _sample_select(scores2d, n_valid):
    rows, width = scores2d.shape
    return pl.pallas_call(
        functools.partial(_sample_select_kernel, n_valid=n_valid),
        out_shape=jax.ShapeDtypeStruct((rows, width), F32),
        grid=(1,),
        in_specs=[pl.BlockSpec((rows, width), lambda i: (0, 0))],
        out_specs=pl.BlockSpec((rows, width), lambda i: (0, 0)),
        scratch_shapes=[pltpu.VMEM((rows, width), F32)],
        compiler_params=pltpu.CompilerParams(dimension_semantics=("arbitrary",),
                                             vmem_limit_bytes=VMEM_LIMIT),
        name="dsa_sample_select",
    )(scores2d)


def _dsa_sample_kernel(pt_ref, qt_ref, knt_ref, vnt_ref, bias_ref, *rest):
    n_ops = DSA_PAGES_PER_STEP
    kpages = rest[:n_ops]
    vpages = rest[n_ops:2 * n_ops]
    o_ref, qc_ref, m_ref, l_ref, acc_ref = rest[2 * n_ops:]
    b = pl.program_id(0)
    j = pl.program_id(1)
    n_steps = pl.num_programs(1)

    @pl.when((b == 0) & (j == 0))
    def _():
        o_ref[...] = jnp.zeros(o_ref.shape, F32)

    @pl.when(j == 0)
    def _():
        qc_ref[...] = _column(qt_ref[...], b)
        m_ref[...] = jnp.full(m_ref.shape, NEG, F32)
        l_ref[...] = jnp.zeros(l_ref.shape, F32)
        acc_ref[...] = jnp.zeros(acc_ref.shape, F32)

    def update(ks, vs, bias_rows):
        qc = qc_ref[...]
        lg = jnp.concatenate([_head_sums(k * qc) + br for k, br in zip(ks, bias_rows)], axis=1)
        m = m_ref[...]
        m_new = jnp.maximum(m, jnp.max(lg, axis=1, keepdims=True))
        alpha = jnp.exp(m - m_new)
        p = jnp.exp(lg - m_new)
        l_ref[...] = alpha * l_ref[...] + jnp.sum(p, axis=1, keepdims=True)
        acc = _head_bcast(alpha) * acc_ref[...]
        for u, v in enumerate(vs):
            acc = acc + _head_bcast(p[:, u * LANES:(u + 1) * LANES]) * v
        acc_ref[...] = acc
        m_ref[...] = m_new

    update([kp[...] for kp in kpages], [vp[...] for vp in vpages],
           [bias_ref[pl.ds(j * n_ops + u, 1), :] for u in range(n_ops)])

    @pl.when(j == n_steps - 1)
    def _():
        lane = lax.broadcasted_iota(I32, (1, LANES), 1)
        bias_row = jnp.where(lane == 0, bias_ref[pl.ds(n_steps * n_ops, 1), :], NEG)
        update([_column(knt_ref[...], b)], [_column(vnt_ref[...], b)], [bias_row])
        o = acc_ref[...] * _head_bcast(1.0 / l_ref[...])
        _set_column(o_ref, b, jnp.sum(o, axis=1, keepdims=True))


def _dsa_sample(page_table, qt, knt, vnt, bias3, cache_k, cache_v):
    dbs, n_pages = page_table.shape
    n_ops = DSA_PAGES_PER_STEP
    n_steps = n_pages // n_ops
    order = lambda j, u: j * n_ops + u
    full = pl.BlockSpec((W_HEADS, LANES), lambda b, j, pt: (0, 0))
    grid_spec = pltpu.PrefetchScalarGridSpec(
        num_scalar_prefetch=1,
        grid=(dbs, n_steps),
        in_specs=[full, full, full,
                  pl.BlockSpec((None, n_pages + 8, LANES), lambda b, j, pt: (b, 0, 0))]
        + _page_specs(n_ops, W_HEADS, order) + _page_specs(n_ops, W_HEADS, order),
        out_specs=full,
        scratch_shapes=[pltpu.VMEM((W_HEADS, LANES), F32), pltpu.VMEM((N_HEADS, 1), F32),
                        pltpu.VMEM((N_HEADS, 1), F32), pltpu.VMEM((W_HEADS, LANES), F32)],
    )
    return pl.pallas_call(
        _dsa_sample_kernel,
        out_shape=jax.ShapeDtypeStruct((W_HEADS, dbs), F32),
        grid_spec=grid_spec,
        compiler_params=pltpu.CompilerParams(dimension_semantics=("arbitrary", "arbitrary"),
                                             vmem_limit_bytes=VMEM_LIMIT),
        name="dsa_sample",
    )(page_table, qt, knt, vnt, bias3, *([cache_k] * n_ops), *([cache_v] * n_ops))


def _sb_sample_kernel(pt_ref, qt_ref, kc_hbm, vc_hbm, o_ref, kbuf, vbuf, sem, qc_ref, out_ref):
    b = pl.program_id(0)
    n_seq = pl.num_programs(0)
    last = pt_ref.shape[1] - 1
    after = _after_matrix(PAGE)

    def copies(seq, page, slot):
        phys = pt_ref[seq, page]
        return (pltpu.make_async_copy(kc_hbm.at[phys], kbuf.at[slot], sem.at[0, slot]),
                pltpu.make_async_copy(vc_hbm.at[phys], vbuf.at[slot], sem.at[1, slot]))

    def start(seq, page, slot):
        for c in copies(seq, page, slot):
            c.start()

    def wait(seq, page, slot):
        for c in copies(seq, page, slot):
            c.wait()

    first_slot = 2 + b % 2

    def slot_of(page):
        return jnp.where(page == last, first_slot, page % 2)

    @pl.when(b == 0)
    def _():
        start(0, last, 2)
        o_ref[...] = jnp.zeros(o_ref.shape, F32)

    @pl.when(b + 1 < n_seq)
    def _():
        start(b + 1, last, 2 + (b + 1) % 2)

    qc_ref[...] = _column(qt_ref[...], b)
    out_ref[...] = jnp.zeros(out_ref.shape, F32)

    def body(carry):
        page, acc_log = carry
        slot = slot_of(page)

        @pl.when(page > 0)
        def _():
            start(b, page - 1, (page - 1) % 2)

        wait(b, page, slot)
        z = _head_sums(kbuf[slot] * qc_ref[...])
        ln = _log_sigmoid_neg(z)
        between = _suffix_sum_excl(ln, after) + acc_log
        a = jnp.exp(z + ln + between)
        out_ref[...] += _head_bcast(a) * vbuf[slot]
        return page - 1, acc_log + jnp.sum(ln, axis=1, keepdims=True)

    def cond(carry):
        page, acc_log = carry
        return (page >= 0) & (jnp.max(acc_log) >= EXP_ZERO_BELOW)

    page_end, _ = lax.while_loop(cond, body, (jnp.int32(last), jnp.zeros((N_HEADS, 1), F32)))

    @pl.when(page_end >= 0)
    def _():
        wait(b, page_end, slot_of(page_end))

    _set_column(o_ref, b, jnp.sum(out_ref[...], axis=1, keepdims=True))


def _sb_sample(page_table, qt, cache_k, cache_v):
    dbs, n_pages = page_table.shape
    full = pl.BlockSpec((W_HEADS, LANES), lambda b, pt: (0, 0))
    grid_spec = pltpu.PrefetchScalarGridSpec(
        num_scalar_prefetch=1,
        grid=(dbs,),
        in_specs=[full, pl.BlockSpec(memory_space=pl.ANY), pl.BlockSpec(memory_space=pl.ANY)],
        out_specs=full,
        scratch_shapes=[pltpu.VMEM((4, W_HEADS, PAGE), F32), pltpu.VMEM((4, W_HEADS, PAGE), F32),
                        pltpu.SemaphoreType.DMA((2, 4)),
                        pltpu.VMEM((W_HEADS, LANES), F32), pltpu.VMEM((W_HEADS, LANES), F32)],
    )
    return pl.pallas_call(
        _sb_sample_kernel,
        out_shape=jax.ShapeDtypeStruct((W_HEADS, dbs), F32),
        grid_spec=grid_spec,
        compiler_params=pltpu.CompilerParams(dimension_semantics=("arbitrary",)),
        name="sb_sample",
    )(page_table, qt, cache_k, cache_v)


def _mix_out_kernel(x_ref, oa_ref, ob_ref, ga_ref, gb_ref, gate_ref, woa_ref, wob_ref, wout_ref, g_ref, o_ref):
    ma = jnp.dot(oa_ref[...], woa_ref[...], preferred_element_type=F32)
    mb = jnp.dot(ob_ref[...], wob_ref[...], preferred_element_type=F32)
    merged = ga_ref[...].astype(F32) * ma + gb_ref[...].astype(F32) * mb
    y = jnp.dot(merged.astype(BF16), wout_ref[...], preferred_element_type=F32)
    o_ref[...] = x_ref[...] + gate_ref[0] * (_rms(y) * g_ref[...])


def _mix_out(x2d, oa, ob, ga, gb, mod3, woa, wob, wout, g_post, tm, rows_per_mod):
    rows = x2d.shape[0]
    r_mod = mod3.shape[1]
    tiles_per_mod = rows_per_mod // tm

    def row_spec(width):
        return pl.BlockSpec((tm, width), lambda i: (i, 0))

    def full(shape):
        return pl.BlockSpec(shape, lambda i: (0,) * len(shape))

    return pl.pallas_call(
        _mix_out_kernel,
        out_shape=jax.ShapeDtypeStruct((rows, D_MODEL), F32),
        grid=(rows // tm,),
        in_specs=[row_spec(D_MODEL), row_spec(W_HEADS), row_spec(W_HEADS), row_spec(D_MODEL), row_spec(D_MODEL),
                  pl.BlockSpec((1, r_mod, D_MODEL), lambda i: (i // tiles_per_mod, 0, 2)),
                  full((W_HEADS, D_MODEL)), full((W_HEADS, D_MODEL)), full((D_MODEL, D_MODEL)),
                  full((1, D_MODEL))],
        out_specs=row_spec(D_MODEL),
        compiler_params=pltpu.CompilerParams(dimension_semantics=("arbitrary",),
                                             vmem_limit_bytes=VMEM_LIMIT),
        name="mix_out",
    )(x2d, oa, ob, ga, gb, mod3, woa, wob, wout, g_post)


def _ffn_kernel(x_ref, shift_ref, scale_ref, gate_ref, gpre_ref, gpost_ref, win_ref, wout_ref, o_ref, *, d_ff, n_split):
    x = x_ref[...]
    h = (_rms(x) * gpre_ref[...] * (1.0 + scale_ref[0]) + shift_ref[0]).astype(BF16)
    part = d_ff // n_split
    y = jnp.zeros(x.shape, F32)
    for s in range(n_split):
        g = jnp.dot(h, win_ref[:, s * part:(s + 1) * part], preferred_element_type=F32)
        u = jnp.dot(h, win_ref[:, d_ff + s * part:d_ff + (s + 1) * part], preferred_element_type=F32)
        act = (g * (1.0 / (1.0 + jnp.exp(-g))) * u).astype(BF16)
        y = y + jnp.dot(act, wout_ref[s * part:(s + 1) * part, :], preferred_element_type=F32)
    o_ref[...] = x + gate_ref[0] * (_rms(y) * gpost_ref[...])


def _ffn(x2d, mod3, g_pre, g_post, w_in, w_out, tm, rows_per_mod):
    rows = x2d.shape[0]
    r_mod = mod3.shape[1]
    tiles_per_mod = rows_per_mod // tm
    d_ff = w_out.shape[0]
    n_split = 2
    assert d_ff % (n_split * LANES) == 0

    def mod_spec(chunk):
        return pl.BlockSpec((1, r_mod, D_MODEL), lambda i: (i // tiles_per_mod, 0, chunk))

    def full(shape):
        return pl.BlockSpec(shape, lambda i: (0,) * len(shape))

    return pl.pallas_call(
        functools.partial(_ffn_kernel, d_ff=d_ff, n_split=n_split),
        out_shape=jax.ShapeDtypeStruct((rows, D_MODEL), F32),
        grid=(rows // tm,),
        in_specs=[pl.BlockSpec((tm, D_MODEL), lambda i: (i, 0)), mod_spec(3), mod_spec(4), mod_spec(5),
                  full((1, D_MODEL)), full((1, D_MODEL)), full((D_MODEL, 2 * d_ff)), full((d_ff, D_MODEL))],
        out_specs=pl.BlockSpec((tm, D_MODEL), lambda i: (i, 0)),
        compiler_params=pltpu.CompilerParams(dimension_semantics=("arbitrary",),
                                             vmem_limit_bytes=VMEM_LIMIT),
        name="ffn",
    )(x2d, mod3, mod3, mod3, g_pre, g_post, w_in, w_out)


def _pack_w_in(w):
    a = W_HEADS
    qa, ka, va, qi = w[:, 0:a], w[:, a:2 * a], w[:, 2 * a:3 * a], w[:, 3 * a:4 * a]
    o = 4 * a
    ki, wi = w[:, o:o + IDX_DIM], w[:, o + IDX_DIM:o + IDX_DIM + N_HEADS]
    o += IDX_DIM + N_HEADS
    qb, kb, vb = w[:, o:o + a], w[:, o + a:o + 2 * a], w[:, o + 2 * a:o + 3 * a]
    o += 3 * a
    ga, gb = w[:, o:o + D_MODEL], w[:, o + D_MODEL:o + 2 * D_MODEL]
    pad = jnp.zeros((w.shape[0], LANES - IDX_DIM - N_HEADS), w.dtype)
    slab_pad = jnp.zeros((w.shape[0], N_HEADS, SLAB - HEAD_DIM), w.dtype)
    vas = jnp.concatenate([va.reshape(-1, N_HEADS, HEAD_DIM), slab_pad], axis=2).reshape(-1, N_HEADS * SLAB)
    return jnp.concatenate([qa, ka, va, vas, qi, ki, wi, pad, qb, kb, vb, ga, gb], axis=1).astype(BF16)


def _rope_tables(pos):
    half = HEAD_DIM // 2
    freqs = ROPE_THETA ** (-jnp.arange(half, dtype=F32) / half)
    ang = pos.astype(F32)[:, None] * freqs[None, :]
    cos, sin = jnp.cos(ang), jnp.sin(ang)
    cos_h = jnp.concatenate([cos, cos], axis=1)
    sin_h = jnp.concatenate([-sin, sin], axis=1)
    cos2 = jnp.concatenate([cos_h, cos_h], axis=1)
    sin2 = jnp.concatenate([sin_h, sin_h], axis=1)
    cosk = jnp.concatenate([cos_h, jnp.ones_like(cos_h)], axis=1)
    sink = jnp.concatenate([sin_h, jnp.zeros_like(sin_h)], axis=1)
    return cos2, sin2, cosk, sink


def _token_minor_pages(cache):
    depth, n_pool = cache.shape[:2]
    return jnp.transpose(cache, (0, 1, 3, 4, 2)).reshape(depth, n_pool, W_HEADS, PAGE)


def kernel(x_prompt, x_sample, c_prompt, c_sample, cache_k_a, cache_v_a, cache_k_idx, cache_k_b, cache_v_b,
           page_table, w_ada, b_ada, g_pre_mix, g_post_mix, w_in, w_o_a, w_o_b, w_out, g_pre_ffn, g_post_ffn,
           w_ffn_in, w_ffn_out):
    depth = w_ada.shape[0]
    assert depth == 1
    bsz, seq, _ = x_prompt.shape
    dbs, dec_seq, _ = x_sample.shape
    assert dec_seq == 1 and dbs == LANES
    n_pages = page_table.shape[1]
    past = n_pages * PAGE
    assert past + dec_seq > TOPK_MAX
    layer = 0

    n_c = bsz + dbs
    c_all = jnp.concatenate([c_prompt, c_sample, jnp.zeros((-n_c % 8, D_MODEL), F32)], axis=0)
    mod = _modulation(c_all, w_ada[layer], b_ada[layer][None, :])
    mod_p = mod[:bsz].reshape(bsz, 1, 6 * D_MODEL)
    mod_s = mod[bsz:n_c].reshape(1, dbs, 6 * D_MODEL)

    w_packed = _pack_w_in(w_in[layer])
    woa, wob, wout = w_o_a[layer].astype(BF16), w_o_b[layer].astype(BF16), w_out[layer].astype(BF16)
    wfi, wfo = w_ffn_in[layer].astype(BF16), w_ffn_out[layer].astype(BF16)
    g1, g2 = g_pre_mix[layer][None, :], g_post_mix[layer][None, :]
    g3, g4 = g_pre_ffn[layer][None, :], g_post_ffn[layer][None, :]

    tm = 256
    xp = x_prompt.reshape(bsz * seq, D_MODEL)
    tabs_p = _rope_tables(jnp.arange(seq, dtype=I32))
    pp = _projection(xp, g1, mod_p, w_packed, tabs_p, tm, seq, seq // tm)
    oa_p = _dsa_prompt(pp, bsz, seq)
    ob_p = _sb_prompt(pp, bsz, seq)
    x1_p = _mix_out(xp, oa_p, ob_p, pp["ga"], pp["gb"], mod_p, woa, wob, wout, g2, tm, seq)
    y_p = _ffn(x1_p, mod_p, g3, g4, wfi, wfo, tm, seq).reshape(bsz, seq, D_MODEL)

    xs = x_sample.reshape(dbs, D_MODEL)
    tabs_s = _rope_tables(jnp.full((dbs,), past, dtype=I32))
    ps = _projection(xs, g1, mod_s, w_packed, tabs_s, dbs, dbs, 1)

    qi3 = ps["qi"].reshape(dbs, N_HEADS, IDX_DIM)
    w3 = ps["kiwf"][:, IDX_DIM:IDX_DIM + N_HEADS].reshape(dbs, N_HEADS, 1)
    kinew3 = ps["kib"][:, :IDX_DIM].reshape(dbs, 1, IDX_DIM)
    cki = jnp.transpose(cache_k_idx, (0, 1, 3, 2))
    scores = _idx_scores(page_table, qi3, w3, kinew3, cki)
    bias = _sample_select(scores.reshape(dbs, (n_pages + 8) * LANES), past + 1)
    bias3 = bias.reshape(dbs, n_pages + 8, LANES)

    qat = ps["qa"].astype(F32).T
    oa_s = _dsa_sample(page_table, qat, ps["kat"][0], ps["vat"][0], bias3,
                       _token_minor_pages(cache_k_a), _token_minor_pages(cache_v_a)).T
    qbt = ps["qb"].astype(F32).T
    ob_s = _sb_sample(page_table, qbt, _token_minor_pages(cache_k_b)[layer],
                      _token_minor_pages(cache_v_b)[layer]).T

    x1_s = _mix_out(xs, oa_s.astype(BF16), ob_s.astype(BF16), ps["ga"], ps["gb"], mod_s,
                    woa, wob, wout, g2, dbs, dbs)
    y_s = _ffn(x1_s, mod_s, g3, g4, wfi, wfo, dbs, dbs).reshape(dbs, 1, D_MODEL)

    def rows_p(t, width):
        if width == IDX_DIM:
            return jnp.transpose(t, (0, 2, 1)).reshape(depth, bsz, seq, IDX_DIM)
        return jnp.transpose(t.reshape(bsz, N_HEADS, HEAD_DIM, seq), (0, 3, 1, 2)).reshape(
            depth, bsz, seq, N_HEADS, HEAD_DIM)

    def rows_s(t, width):
        if width == IDX_DIM:
            return jnp.transpose(t[0], (1, 0)).reshape(depth, dbs, dec_seq, IDX_DIM)
        return jnp.transpose(t[0].reshape(N_HEADS, HEAD_DIM, dbs), (2, 0, 1)).reshape(
            depth, dbs, dec_seq, N_HEADS, HEAD_DIM)

    return (y_p, y_s,
            rows_p(pp["kat"], W_HEADS), rows_p(pp["vat"], W_HEADS), rows_p(pp["kit"], IDX_DIM),
            rows_p(pp["kbt"], W_HEADS), rows_p(pp["vbt"], W_HEADS),
            rows_s(ps["kat"], W_HEADS), rows_s(ps["vat"], W_HEADS), rows_s(ps["kit"], IDX_DIM),
            rows_s(ps["kbt"], W_HEADS), rows_s(ps["vbt"], W_HEADS))
```

```python
import functools

import jax
import jax.numpy as jnp
from jax import lax
from jax.experimental import pallas as pl
from jax.experimental.pallas import tpu as pltpu

F32 = jnp.float32
BF16 = jnp.bfloat16
I32 = jnp.int32

D_MODEL = 1024
HEAD_DIM = 64
N_HEADS = 8
W_HEADS = N_HEADS * HEAD_DIM
IDX_DIM = 64
TOPK_MAX = 256
ROPE_THETA = 10000.0
RMS_EPS = 1e-6
PAGE = 128

LANES = 128
INT_MIN = -(2**31)
NEG = -1e30
EXP_ZERO_BELOW = -104.0
VMEM_LIMIT = 56 * 1024 * 1024

TQ = 256
TK_SB = 128
CK = 512
SLAB = 128
HEAD_GROUP = 4
SB_HEAD_GROUP = 8
DSA_PAGES_PER_STEP = 8
IDX_PAGES_PER_STEP = 32

_SEC = {}
_off = 0
for _name, _w in (("qa", 512), ("ka", 512), ("va", 512), ("vas", N_HEADS * SLAB), ("qi", 512), ("kiw", 128),
                  ("qb", 512), ("kb", 512), ("vb", 512), ("ga", 1024), ("gb", 1024)):
    _SEC[_name] = (_off, _off + _w)
    _off += _w
W_IN_PACKED = _off


def _nt_dot(a, b):
    return lax.dot_general(a, b, (((1,), (1,)), ((), ())), preferred_element_type=F32)


def _rms(x):
    return x * lax.rsqrt(jnp.mean(x * x, axis=-1, keepdims=True) + RMS_EPS)


def _mod_kernel(c_ref, w_ref, b_ref, o_ref):
    c = c_ref[...]
    s = c * (1.0 / (1.0 + jnp.exp(-c)))
    o_ref[...] = jnp.dot(s, w_ref[...], preferred_element_type=F32,
                         precision=lax.Precision.HIGHEST) + b_ref[...]


def _modulation(c_all, w_ada, b_ada):
    m = c_all.shape[0]
    n = w_ada.shape[1]
    tn = 512
    return pl.pallas_call(
        _mod_kernel,
        out_shape=jax.ShapeDtypeStruct((m, n), F32),
        grid=(n // tn,),
        in_specs=[pl.BlockSpec((m, D_MODEL), lambda j: (0, 0)),
                  pl.BlockSpec((D_MODEL, tn), lambda j: (0, j)),
                  pl.BlockSpec((1, tn), lambda j: (0, j))],
        out_specs=pl.BlockSpec((m, tn), lambda j: (0, j)),
        compiler_params=pltpu.CompilerParams(dimension_semantics=("arbitrary",)),
        name="adaln_mod",
    )(c_all, w_ada, b_ada)


def _proj_kernel(x_ref, g_ref, shift_ref, scale_ref, w_ref, cos_ref, sin_ref, cosk_ref, sink_ref,
                 qa_ref, vas_ref, qi_ref, kiwf_ref, kib_ref, qb_ref, vbb_ref, ga_ref, gb_ref,
                 kat_ref, vat_ref, kit_ref, kbt_ref, vbt_ref, katb_ref, kitb_ref, kbtb_ref):
    tm = x_ref.shape[0]
    h = (_rms(x_ref[...]) * g_ref[...] * (1.0 + scale_ref[0]) + shift_ref[0]).astype(BF16)

    def mm(name):
        a, b = _SEC[name]
        return jnp.dot(h, w_ref[:, a:b], preferred_element_type=F32)

    lane = lax.broadcasted_iota(I32, (tm, LANES), 1)
    first_half = (lane & 32) == 0

    def rope(p, cos, sin):
        outs = []
        for c in range(p.shape[1] // LANES):
            xc = p[:, c * LANES:(c + 1) * LANES]
            partner = jnp.where(first_half, pltpu.roll(xc, LANES - 32, 1), pltpu.roll(xc, 32, 1))
            outs.append(xc * cos + partner * sin)
        return outs[0] if len(outs) == 1 else jnp.concatenate(outs, axis=1)

    cos = cos_ref[...]
    sin = sin_ref[...]
    scale = HEAD_DIM ** -0.5
    qa_ref[...] = (rope(mm("qa"), cos, sin) * scale).astype(BF16)
    kat = rope(mm("ka"), cos, sin).T
    kat_ref[...] = kat
    katb_ref[...] = kat.astype(BF16)
    vat_ref[...] = mm("va").T
    vas = mm("vas")
    slab_lane = lax.broadcasted_iota(I32, vas.shape, 1) & (SLAB - 1)
    vas_ref[...] = jnp.where(slab_lane == HEAD_DIM, 1.0, vas).astype(BF16)
    qi_ref[...] = (rope(mm("qi"), cos, sin) * (IDX_DIM ** -0.5)).astype(BF16)
    kiw = rope(mm("kiw"), cosk_ref[...], sink_ref[...])
    kiwf_ref[...] = kiw
    kib_ref[...] = kiw.astype(BF16)
    kit = kiw.T[:IDX_DIM, :]
    kit_ref[...] = kit
    kitb_ref[...] = kit.astype(BF16)
    qb_ref[...] = (mm("qb") * scale).astype(BF16)
    kbt = mm("kb").T
    kbt_ref[...] = kbt
    kbtb_ref[...] = kbt.astype(BF16)
    vb = mm("vb")
    vbb_ref[...] = vb.astype(BF16)
    vbt_ref[...] = vb.T
    ga = mm("ga")
    ga_ref[...] = (1.0 / (1.0 + jnp.exp(-ga))).astype(BF16)
    gb = mm("gb")
    gb_ref[...] = (1.0 / (1.0 + jnp.exp(-gb))).astype(BF16)


def _projection(x2d, g_pre, mod3, w_packed, tabs, tm, rows_per_mod, n_pos_tiles):
    rows = x2d.shape[0]
    n_tiles = rows // tm
    r_mod = mod3.shape[1]
    tiles_per_mod = rows_per_mod // tm
    groups = rows // rows_per_mod

    def row_spec(width):
        return pl.BlockSpec((tm, width), lambda i: (i, 0))

    def t_spec(width):
        return pl.BlockSpec((None, width, tm), lambda i: (i // tiles_per_mod, 0, i % tiles_per_mod))

    def mod_spec(chunk):
        return pl.BlockSpec((1, r_mod, D_MODEL), lambda i: (i // tiles_per_mod, 0, chunk))

    tab_spec = pl.BlockSpec((tm, LANES), lambda i: (i % n_pos_tiles, 0))
    outs = [("qa", 512, BF16), ("vas", N_HEADS * SLAB, BF16),
            ("qi", 512, BF16), ("kiwf", 128, F32), ("kib", 128, BF16), ("qb", 512, BF16),
            ("vbb", 512, BF16), ("ga", 1024, BF16), ("gb", 1024, BF16)]
    outs_t = [("kat", W_HEADS, F32), ("vat", W_HEADS, F32), ("kit", IDX_DIM, F32), ("kbt", W_HEADS, F32),
              ("vbt", W_HEADS, F32), ("katb", W_HEADS, BF16), ("kitb", IDX_DIM, BF16), ("kbtb", W_HEADS, BF16)]
    res = pl.pallas_call(
        _proj_kernel,
        out_shape=[jax.ShapeDtypeStruct((rows, w), dt) for _, w, dt in outs]
        + [jax.ShapeDtypeStruct((groups, w, rows_per_mod), dt) for _, w, dt in outs_t],
        grid=(n_tiles,),
        in_specs=[row_spec(D_MODEL),
                  pl.BlockSpec((1, D_MODEL), lambda i: (0, 0)),
                  mod_spec(0), mod_spec(1),
                  pl.BlockSpec((D_MODEL, W_IN_PACKED), lambda i: (0, 0)),
                  tab_spec, tab_spec, tab_spec, tab_spec],
        out_specs=[row_spec(w) for _, w, _ in outs] + [t_spec(w) for _, w, _ in outs_t],
        compiler_params=pltpu.CompilerParams(dimension_semantics=("arbitrary",),
                                             vmem_limit_bytes=VMEM_LIMIT),
        name="in_proj",
    )(x2d, g_pre, mod3, mod3, w_packed, *tabs)
    return {n: r for n, r in zip([o[0] for o in outs] + [o[0] for o in outs_t], res)}


def _masked_scores(acc, valid):
    return jnp.where(valid, acc, -jnp.inf)


def _key_to_float(key):
    bits = jnp.where(key < 0, (key - 1) ^ jnp.int32(0x7FFFFFFF), key)
    return pltpu.bitcast(bits, F32)


def _lane_fold(x):
    out = x[:, :LANES]
    for q in range(1, x.shape[1] // LANES):
        out = out + x[:, q * LANES:(q + 1) * LANES]
    return out


def _select_topk(score_ref, bias_ref, n_chunks, k):
    rows = score_ref.shape[0]
    kf = float(k)

    def chunk_slice(c):
        return pl.ds(pl.multiple_of(c * CK, CK), CK)

    def count_ge(cand):
        def chunk(c, cnt):
            return cnt + _lane_fold(jnp.where(score_ref[:, chunk_slice(c)] >= cand, 1.0, 0.0))
        cnt = lax.fori_loop(0, n_chunks, chunk, jnp.zeros((rows, LANES), F32))
        return jnp.sum(cnt, axis=1, keepdims=True)

    def bit_step(it, tu):
        cand_u = tu | jnp.left_shift(jnp.int32(1), 31 - it)
        cnt = count_ge(_key_to_float(cand_u ^ jnp.int32(INT_MIN)))
        return jnp.where(cnt >= kf, cand_u, tu)

    tu = lax.fori_loop(0, 32, bit_step, jnp.zeros((rows, 1), I32))
    ts = jnp.where(tu == 0, -jnp.inf, _key_to_float(tu ^ jnp.int32(INT_MIN)))

    def tally(c, carry):
        cge, cgt = carry
        sc = score_ref[:, chunk_slice(c)]
        ge = sc >= ts
        bias_ref[:, chunk_slice(c)] = jnp.where(ge & (sc > -jnp.inf), 0.0, NEG)
        return (cge + _lane_fold(jnp.where(ge, 1.0, 0.0)), cgt + _lane_fold(jnp.where(sc > ts, 1.0, 0.0)))

    zero = jnp.zeros((rows, LANES), F32)
    cge, cgt = lax.fori_loop(0, n_chunks, tally, (zero, zero))
    cnt_ge = jnp.sum(cge, axis=1, keepdims=True)
    cnt_gt = jnp.sum(cgt, axis=1, keepdims=True)
    tie = (cnt_ge > kf) & (ts > -jnp.inf)

    @pl.when(jnp.max(jnp.where(tie, 1.0, 0.0)) > 0.0)
    def _():
        need = kf - cnt_gt
        r = lax.broadcasted_iota(I32, (LANES, LANES), 0)
        c_ = lax.broadcasted_iota(I32, (LANES, LANES), 1)
        before = jnp.where(r < c_, 1.0, 0.0).astype(BF16)

        def chunk(c, run):
            sl = pl.ds(pl.multiple_of(c * LANES, LANES), LANES)
            sc = score_ref[:, sl]
            eq = jnp.where(sc == ts, 1.0, 0.0)
            rank = jnp.dot(eq.astype(BF16), before, preferred_element_type=F32) + run
            take = (sc > ts) | ((sc == ts) & (rank < need))
            bias_ref[:, sl] = jnp.where(tie, jnp.where(take, 0.0, NEG), bias_ref[:, sl])
            return run + jnp.sum(eq, axis=1, keepdims=True)

        lax.fori_loop(0, n_chunks * (CK // LANES), chunk, jnp.zeros((rows, 1), F32))


def _dsa_prompt_kernel(qi_ref, kiwq_ref, kit_ref, qa_ref, kat_ref, vas_ref, o_ref, score_ref, bias_ref):
    i = pl.program_id(1)
    n_chunks = i // (CK // TQ) + 1
    qi = qi_ref[...]
    w = kiwq_ref[:, IDX_DIM:IDX_DIM + N_HEADS] * (N_HEADS ** -0.5)
    qih = [qi[:, h * IDX_DIM:(h + 1) * IDX_DIM] for h in range(N_HEADS)]
    wh = [w[:, h:h + 1] for h in range(N_HEADS)]
    t_idx = i * TQ + lax.broadcasted_iota(I32, (TQ, CK), 0)
    col = lax.broadcasted_iota(I32, (TQ, CK), 1)

    def score_chunk(c, _):
        start = pl.multiple_of(c * CK, CK)
        kic = kit_ref[:, pl.ds(start, CK)]
        acc = jnp.zeros((TQ, CK), F32)
        for h in range(N_HEADS):
            s = jnp.dot(qih[h], kic, preferred_element_type=F32)
            acc = acc + jnp.maximum(s, 0.0) * wh[h]
        score_ref[:, pl.ds(start, CK)] = _masked_scores(acc, (col + start) <= t_idx)
        return 0

    lax.fori_loop(0, n_chunks, score_chunk, 0)
    _select_topk(score_ref, bias_ref, n_chunks, TOPK_MAX)

    qa = qa_ref[...]
    qah = [qa[:, h * HEAD_DIM:(h + 1) * HEAD_DIM] for h in range(N_HEADS)]

    for h0 in range(0, N_HEADS, HEAD_GROUP):
        heads = range(h0, h0 + HEAD_GROUP)

        def att_chunk(c, carry, heads=heads):
            ms, accs = carry
            start = pl.multiple_of(c * CK, CK)
            bias = bias_ref[:, pl.ds(start, CK)]
            ms2, accs2 = [], []
            for n, h in enumerate(heads):
                kc = kat_ref[h * HEAD_DIM:(h + 1) * HEAD_DIM, pl.ds(start, CK)]
                lg = jnp.dot(qah[h], kc, preferred_element_type=F32) + bias
                m_new = jnp.maximum(ms[n], jnp.max(lg, axis=1, keepdims=True))
                alpha = jnp.exp(ms[n] - m_new)
                p = jnp.exp(lg - m_new).astype(BF16)
                pv = jnp.dot(p, vas_ref[pl.ds(start, CK), h * SLAB:(h + 1) * SLAB], preferred_element_type=F32)
                ms2.append(m_new)
                accs2.append(alpha * accs[n] + pv)
            return tuple(ms2), tuple(accs2)

        init = (tuple(jnp.full((TQ, 1), NEG, F32) for _ in heads),
                tuple(jnp.zeros((TQ, SLAB), F32) for _ in heads))
        _, accs = lax.fori_loop(0, n_chunks, att_chunk, init)
        for n, h in enumerate(heads):
            o = accs[n][:, :HEAD_DIM] / accs[n][:, HEAD_DIM:HEAD_DIM + 1]
            o_ref[:, h * HEAD_DIM:(h + 1) * HEAD_DIM] = o.astype(o_ref.dtype)


def _dsa_prompt(p, bsz, seq):
    nq = seq // TQ

    def q_spec(width):
        return pl.BlockSpec((TQ, width), lambda b, i: (b * nq + i, 0))

    def batch_spec(rows, width):
        return pl.BlockSpec((None, rows, width), lambda b, i: (b, 0, 0))

    return pl.pallas_call(
        _dsa_prompt_kernel,
        out_shape=jax.ShapeDtypeStruct((bsz * seq, W_HEADS), BF16),
        grid=(bsz, nq),
        in_specs=[q_spec(W_HEADS), q_spec(LANES), batch_spec(IDX_DIM, seq), q_spec(W_HEADS),
                  batch_spec(W_HEADS, seq), batch_spec(seq, N_HEADS * SLAB)],
        out_specs=q_spec(W_HEADS),
        scratch_shapes=[pltpu.VMEM((TQ, seq), F32), pltpu.VMEM((TQ, seq), F32)],
        compiler_params=pltpu.CompilerParams(dimension_semantics=("arbitrary", "arbitrary"),
                                             vmem_limit_bytes=VMEM_LIMIT),
        name="dsa_prompt",
    )(p["qi"], p["kiwf"], p["kitb"], p["qa"], p["katb"], p["vas"].reshape(bsz, seq, N_HEADS * SLAB))


def _log_sigmoid_neg(z):
    return jnp.minimum(-z, 0.0) - jnp.log1p(jnp.exp(-jnp.abs(z)))


def _suffix_sum_excl(lg, after):
    hi = lg.astype(BF16)
    lo = (lg - hi.astype(F32)).astype(BF16)
    return (jnp.dot(hi, after, preferred_element_type=F32)
            + jnp.dot(lo, after, preferred_element_type=F32))


def _after_matrix(n):
    r = lax.broadcasted_iota(I32, (n, n), 0)
    c = lax.broadcasted_iota(I32, (n, n), 1)
    return jnp.where(r > c, 1.0, 0.0).astype(BF16)


def _sb_prompt_kernel(qb_ref, kbt_ref, vb_ref, o_ref):
    i = pl.program_id(1)
    after = _after_matrix(TK_SB)
    t_idx = i * TQ + lax.broadcasted_iota(I32, (TQ, TK_SB), 0)
    col = lax.broadcasted_iota(I32, (TQ, TK_SB), 1)
    qb = qb_ref[...]
    last_block = (i + 1) * (TQ // TK_SB) - 1

    for h0 in range(0, N_HEADS, SB_HEAD_GROUP):
        heads = range(h0, h0 + SB_HEAD_GROUP)
        qbh = [qb[:, h * HEAD_DIM:(h + 1) * HEAD_DIM] for h in heads]

        def body(carry, heads=heads, qbh=qbh):
            j, logs, outs = carry
            start = pl.multiple_of(j * TK_SB, TK_SB)
            strict = (col + start) < t_idx
            logs2, outs2 = [], []
            for n, h in enumerate(heads):
                hs = slice(h * HEAD_DIM, (h + 1) * HEAD_DIM)
                z = jnp.dot(qbh[n], kbt_ref[hs, pl.ds(start, TK_SB)], preferred_element_type=F32)
                lraw = _log_sigmoid_neg(z)
                lg = jnp.where(strict, lraw, 0.0)
                between = _suffix_sum_excl(lg, after) + logs[n]
                a = jnp.where(strict, jnp.exp(z + lraw + between), 0.0)
                outs2.append(outs[n] + jnp.dot(a.astype(BF16), vb_ref[pl.ds(start, TK_SB), hs],
                                               preferred_element_type=F32))
                logs2.append(logs[n] + jnp.sum(lg, axis=1, keepdims=True))
            return j - 1, tuple(logs2), tuple(outs2)

        def cond(carry):
            j, logs, _ = carry
            live = logs[0]
            for lg in logs[1:]:
                live = jnp.maximum(live, lg)
            return (j >= 0) & (jnp.max(live) >= EXP_ZERO_BELOW)

        init = (last_block, tuple(jnp.zeros((TQ, 1), F32) for _ in heads),
                tuple(jnp.zeros((TQ, HEAD_DIM), F32) for _ in heads))
        _, _, outs = lax.while_loop(cond, body, init)
        for n, h in enumerate(heads):
            o_ref[:, h * HEAD_DIM:(h + 1) * HEAD_DIM] = outs[n].astype(o_ref.dtype)


def _sb_prompt(p, bsz, seq):
    nq = seq // TQ
    q_spec = pl.BlockSpec((TQ, W_HEADS), lambda b, i: (b * nq + i, 0))
    return pl.pallas_call(
        _sb_prompt_kernel,
        out_shape=jax.ShapeDtypeStruct((bsz * seq, W_HEADS), BF16),
        grid=(bsz, nq),
        in_specs=[q_spec, pl.BlockSpec((None, W_HEADS, seq), lambda b, i: (b, 0, 0)),
                  pl.BlockSpec((None, seq, W_HEADS), lambda b, i: (b, 0, 0))],
        out_specs=q_spec,
        compiler_params=pltpu.CompilerParams(dimension_semantics=("arbitrary", "arbitrary"),
                                             vmem_limit_bytes=VMEM_LIMIT),
        name="sb_prompt",
    )(p["qb"], p["kbtb"], p["vbb"].reshape(bsz, seq, W_HEADS))


def _head_sums(x):
    return jnp.sum(x.reshape(N_HEADS, HEAD_DIM, x.shape[-1]), axis=1)


def _head_bcast(x):
    return jnp.broadcast_to(x[:, None, :], (N_HEADS, HEAD_DIM, LANES)).reshape(W_HEADS, LANES)


def _column(x, b):
    lane = lax.broadcasted_iota(I32, x.shape, 1)
    col = jnp.sum(jnp.where(lane == b, x, 0.0), axis=1, keepdims=True)
    return jnp.broadcast_to(col, x.shape)


def _set_column(ref, b, col):
    lane = lax.broadcasted_iota(I32, ref.shape, 1)
    ref[...] = jnp.where(lane == b, jnp.broadcast_to(col, ref.shape), ref[...])


def _page_specs(n_ops, rows, order):
    return [pl.BlockSpec((None, None, rows, PAGE),
                         lambda b, j, pt, u=u: (0, pt[b, order(j, u)], 0, 0))
            for u in range(n_ops)]


def _idx_scores_kernel(pt_ref, qi_ref, w_ref, kinew_ref, *rest):
    pages = rest[:IDX_PAGES_PER_STEP]
    o_ref = rest[IDX_PAGES_PER_STEP]
    j = pl.program_id(1)
    qi = qi_ref[...]
    w = w_ref[...] * (N_HEADS ** -0.5)

    keys = jnp.concatenate([pg[...].astype(BF16) for pg in pages], axis=1)
    s = jnp.dot(qi, keys, preferred_element_type=F32)
    r = jnp.sum(jnp.maximum(s, 0.0) * w, axis=0, keepdims=True)
    for u in range(IDX_PAGES_PER_STEP):
        o_ref[pl.ds(j * IDX_PAGES_PER_STEP + u, 1), :] = r[:, u * PAGE:(u + 1) * PAGE]

    @pl.when(j == pl.num_programs(1) - 1)
    def _():
        s_new = jnp.sum(qi.astype(F32) * kinew_ref[...].astype(F32), axis=1, keepdims=True)
        r_new = jnp.sum(jnp.maximum(s_new, 0.0) * w, axis=0, keepdims=True)
        n_slots = pl.num_programs(1) * IDX_PAGES_PER_STEP
        o_ref[pl.ds(n_slots, 8), :] = jnp.broadcast_to(r_new, (8, LANES))


def _idx_scores(page_table, qi3, w3, kinew3, cache_ki_t):
    dbs, n_pages = page_table.shape
    n_steps = n_pages // IDX_PAGES_PER_STEP
    grid_spec = pltpu.PrefetchScalarGridSpec(
        num_scalar_prefetch=1,
        grid=(dbs, n_steps),
        in_specs=[pl.BlockSpec((None, N_HEADS, IDX_DIM), lambda b, j, pt: (b, 0, 0)),
                  pl.BlockSpec((None, N_HEADS, 1), lambda b, j, pt: (b, 0, 0)),
                  pl.BlockSpec((None, 1, IDX_DIM), lambda b, j, pt: (b, 0, 0))]
        + _page_specs(IDX_PAGES_PER_STEP, IDX_DIM, lambda j, u: j * IDX_PAGES_PER_STEP + u),
        out_specs=pl.BlockSpec((None, n_pages + 8, LANES), lambda b, j, pt: (b, 0, 0)),
    )
    return pl.pallas_call(
        _idx_scores_kernel,
        out_shape=jax.ShapeDtypeStruct((dbs, n_pages + 8, LANES), F32),
        grid_spec=grid_spec,
        compiler_params=pltpu.CompilerParams(dimension_semantics=("arbitrary", "arbitrary")),
        name="dsa_sample_scores",
    )(page_table, qi3, w3, kinew3, *([cache_ki_t] * IDX_PAGES_PER_STEP))


def _sample_select_kernel(s_ref, bias_ref, score_ref, *, n_valid):
    rows, width = s_ref.shape
    col = lax.broadcasted_iota(I32, (rows, CK), 1)
    for c in range(width // CK):
        sl = slice(c * CK, (c + 1) * CK)
        score_ref[:, sl] = _masked_scores(s_ref[:, sl], (col + c * CK) < n_valid)
    _select_topk(score_ref, bias_ref, width // CK, TOPK_MAX)


def _sample_select(scores2d, n_valid):
    rows, width = scores2d.shape
    return pl.pallas_call(
        functools.partial(_sample_select_kernel, n_valid=n_valid),
        out_shape=jax.ShapeDtypeStruct((rows, width), F32),
        grid=(1,),
        in_specs=[pl.BlockSpec((rows, width), lambda i: (0, 0))],
        out_specs=pl.BlockSpec((rows, width), lambda i: (0, 0)),
        scratch_shapes=[pltpu.VMEM((rows, width), F32)],
        compiler_params=pltpu.CompilerParams(dimension_semantics=("arbitrary",),
                                             vmem_limit_bytes=VMEM_LIMIT),
        name="dsa_sample_select",
    )(scores2d)


def _head_block_mask():
    r = lax.broadcasted_iota(I32, (N_HEADS, W_HEADS), 0)
    c = lax.broadcasted_iota(I32, (N_HEADS, W_HEADS), 1)
    return (c // HEAD_DIM) == r


def _dsa_sample_kernel(pt_ref, q_ref, knt_ref, vnt_ref, bias_ref, *rest):
    n_ops = DSA_PAGES_PER_STEP
    kpages = rest[:n_ops]
    vpages = rest[n_ops:2 * n_ops]
    o_ref, m_ref, l_ref, acc_ref = rest[2 * n_ops:]
    b = pl.program_id(0)
    j = pl.program_id(1)
    n_steps = pl.num_programs(1)
    blk = _head_block_mask()
    q_bd = jnp.where(blk, jnp.broadcast_to(q_ref[...].astype(F32), (N_HEADS, W_HEADS)), 0.0).astype(BF16)

    @pl.when(j == 0)
    def _():
        m_ref[...] = jnp.full(m_ref.shape, NEG, F32)
        l_ref[...] = jnp.zeros(l_ref.shape, F32)
        acc_ref[...] = jnp.zeros(acc_ref.shape, F32)

    def update(ks, vs, bias_rows):
        lg = jnp.concatenate([jnp.dot(q_bd, k.astype(BF16), preferred_element_type=F32) + br
                              for k, br in zip(ks, bias_rows)], axis=1)
        m = m_ref[...]
        m_new = jnp.maximum(m, jnp.max(lg, axis=1, keepdims=True))
        alpha = jnp.exp(m - m_new)
        p = jnp.exp(lg - m_new)
        l_ref[...] = alpha * l_ref[...] + jnp.sum(p, axis=1, keepdims=True)
        acc = alpha * acc_ref[...]
        for u, v in enumerate(vs):
            acc = acc + _nt_dot(p[:, u * LANES:(u + 1) * LANES].astype(BF16), v.astype(BF16))
        acc_ref[...] = acc
        m_ref[...] = m_new

    update([kp[...] for kp in kpages], [vp[...] for vp in vpages],
           [bias_ref[pl.ds(j * n_ops + u, 1), :] for u in range(n_ops)])

    @pl.when(j == n_steps - 1)
    def _():
        lane = lax.broadcasted_iota(I32, (1, LANES), 1)
        bias_row = jnp.where(lane == 0, bias_ref[pl.ds(n_steps * n_ops, 1), :], NEG)
        update([_column(knt_ref[...], b)], [_column(vnt_ref[...], b)], [bias_row])
        o = acc_ref[...] / l_ref[...]
        o_ref[...] = jnp.sum(jnp.where(blk, o, 0.0), axis=0, keepdims=True).astype(o_ref.dtype)


def _dsa_sample(page_table, q3, knt, vnt, bias3, cache_k, cache_v):
    dbs, n_pages = page_table.shape
    n_ops = DSA_PAGES_PER_STEP
    n_steps = n_pages // n_ops
    order = lambda j, u: j * n_ops + u
    full = pl.BlockSpec((W_HEADS, LANES), lambda b, j, pt: (0, 0))
    row = pl.BlockSpec((None, 1, W_HEADS), lambda b, j, pt: (b, 0, 0))
    grid_spec = pltpu.PrefetchScalarGridSpec(
        num_scalar_prefetch=1,
        grid=(dbs, n_steps),
        in_specs=[row, full, full,
                  pl.BlockSpec((None, n_pages + 8, LANES), lambda b, j, pt: (b, 0, 0))]
        + _page_specs(n_ops, W_HEADS, order) + _page_specs(n_ops, W_HEADS, order),
        out_specs=row,
        scratch_shapes=[pltpu.VMEM((N_HEADS, 1), F32), pltpu.VMEM((N_HEADS, 1), F32),
                        pltpu.VMEM((N_HEADS, W_HEADS), F32)],
    )
    return pl.pallas_call(
        _dsa_sample_kernel,
        out_shape=jax.ShapeDtypeStruct((dbs, 1, W_HEADS), BF16),
        grid_spec=grid_spec,
        compiler_params=pltpu.CompilerParams(dimension_semantics=("arbitrary", "arbitrary"),
                                             vmem_limit_bytes=VMEM_LIMIT),
        name="dsa_sample",
    )(page_table, q3, knt, vnt, bias3, *([cache_k] * n_ops), *([cache_v] * n_ops))


def _sb_sample_kernel(pt_ref, qt_ref, kc_hbm, vc_hbm, o_ref, kbuf, vbuf, sem, qc_ref, out_ref):
    b = pl.program_id(0)
    n_seq = pl.num_programs(0)
    last = pt_ref.shape[1] - 1
    after = _after_matrix(PAGE)

    def copies(seq, page, slot):
        phys = pt_ref[seq, page]
        return (pltpu.make_async_copy(kc_hbm.at[phys], kbuf.at[slot], sem.at[0, slot]),
                pltpu.make_async_copy(vc_hbm.at[phys], vbuf.at[slot], sem.at[1, slot]))

    def start(seq, page, slot):
        for c in copies(seq, page, slot):
            c.start()

    def wait(seq, page, slot):
        for c in copies(seq, page, slot):
            c.wait()

    first_slot = 2 + b % 2

    def slot_of(page):
        return jnp.where(page == last, first_slot, page % 2)

    @pl.when(b == 0)
    def _():
        start(0, last, 2)
        o_ref[...] = jnp.zeros(o_ref.shape, F32)

    @pl.when(b + 1 < n_seq)
    def _():
        start(b + 1, last, 2 + (b + 1) % 2)

    qc_ref[...] = _column(qt_ref[...], b)
    out_ref[...] = jnp.zeros(out_ref.shape, F32)

    def body(carry):
        page, acc_log = carry
        slot = slot_of(page)

        @pl.when(page > 0)
        def _():
            start(b, page - 1, (page - 1) % 2)

        wait(b, page, slot)
        z = _head_sums(kbuf[slot] * qc_ref[...])
        ln = _log_sigmoid_neg(z)
        between = _suffix_sum_excl(ln, after) + acc_log
        a = jnp.exp(z + ln + between)
        out_ref[...] += _head_bcast(a) * vbuf[slot]
        return page - 1, acc_log + jnp.sum(ln, axis=1, keepdims=True)

    def cond(carry):
        page, acc_log = carry
        return (page >= 0) & (jnp.max(acc_log) >= EXP_ZERO_BELOW)

    page_end, _ = lax.while_loop(cond, body, (jnp.int32(last), jnp.zeros((N_HEADS, 1), F32)))

    @pl.when(page_end >= 0)
    def _():
        wait(b, page_end, slot_of(page_end))

    _set_column(o_ref, b, jnp.sum(out_ref[...], axis=1, keepdims=True))


def _sb_sample(page_table, qt, cache_k, cache_v):
    dbs, n_pages = page_table.shape
    full = pl.BlockSpec((W_HEADS, LANES), lambda b, pt: (0, 0))
    grid_spec = pltpu.PrefetchScalarGridSpec(
        num_scalar_prefetch=1,
        grid=(dbs,),
        in_specs=[full, pl.BlockSpec(memory_space=pl.ANY), pl.BlockSpec(memory_space=pl.ANY)],
        out_specs=full,
        scratch_shapes=[pltpu.VMEM((4, W_HEADS, PAGE), F32), pltpu.VMEM((4, W_HEADS, PAGE), F32),
                        pltpu.SemaphoreType.DMA((2, 4)),
                        pltpu.VMEM((W_HEADS, LANES), F32), pltpu.VMEM((W_HEADS, LANES), F32)],
    )
    return pl.pallas_call(
        _sb_sample_kernel,
        out_shape=jax.ShapeDtypeStruct((W_HEADS, dbs), F32),
        grid_spec=grid_spec,
        compiler_params=pltpu.CompilerParams(dimension_semantics=("arbitrary",)),
        name="sb_sample",
    )(page_table, qt, cache_k, cache_v)


def _mix_out_kernel(x_ref, oa_ref, ob_ref, ga_ref, gb_ref, gate_ref, woa_ref, wob_ref, wout_ref, g_ref, o_ref):
    ma = jnp.dot(oa_ref[...], woa_ref[...], preferred_element_type=F32)
    mb = jnp.dot(ob_ref[...], wob_ref[...], preferred_element_type=F32)
    merged = ga_ref[...].astype(F32) * ma + gb_ref[...].astype(F32) * mb
    y = jnp.dot(merged.astype(BF16), wout_ref[...], preferred_element_type=F32)
    o_ref[...] = x_ref[...] + gate_ref[0] * (_rms(y) * g_ref[...])


def _mix_out(x2d, oa, ob, ga, gb, mod3, woa, wob, wout, g_post, tm, rows_per_mod):
    rows = x2d.shape[0]
    r_mod = mod3.shape[1]
    tiles_per_mod = rows_per_mod // tm

    def row_spec(width):
        return pl.BlockSpec((tm, width), lambda i: (i, 0))

    def full(shape):
        return pl.BlockSpec(shape, lambda i: (0,) * len(shape))

    return pl.pallas_call(
        _mix_out_kernel,
        out_shape=jax.ShapeDtypeStruct((rows, D_MODEL), F32),
        grid=(rows // tm,),
        in_specs=[row_spec(D_MODEL), row_spec(W_HEADS), row_spec(W_HEADS), row_spec(D_MODEL), row_spec(D_MODEL),
                  pl.BlockSpec((1, r_mod, D_MODEL), lambda i: (i // tiles_per_mod, 0, 2)),
                  full((W_HEADS, D_MODEL)), full((W_HEADS, D_MODEL)), full((D_MODEL, D_MODEL)),
                  full((1, D_MODEL))],
        out_specs=row_spec(D_MODEL),
        compiler_params=pltpu.CompilerParams(dimension_semantics=("arbitrary",),
                                             vmem_limit_bytes=VMEM_LIMIT),
        name="mix_out",
    )(x2d, oa, ob, ga, gb, mod3, woa, wob, wout, g_post)


def _ffn_kernel(x_ref, shift_ref, scale_ref, gate_ref, gpre_ref, gpost_ref, win_ref, wout_ref, o_ref, *, d_ff, n_split):
    x = x_ref[...]
    h = (_rms(x) * gpre_ref[...] * (1.0 + scale_ref[0]) + shift_ref[0]).astype(BF16)
    part = d_ff // n_split
    y = jnp.zeros(x.shape, F32)
    for s in range(n_split):
        g = jnp.dot(h, win_ref[:, s * part:(s + 1) * part], preferred_element_type=F32)
        u = jnp.dot(h, win_ref[:, d_ff + s * part:d_ff + (s + 1) * part], preferred_element_type=F32)
        act = (g * (1.0 / (1.0 + jnp.exp(-g))) * u).astype(BF16)
        y = y + jnp.dot(act, wout_ref[s * part:(s + 1) * part, :], preferred_element_type=F32)
    o_ref[...] = x + gate_ref[0] * (_rms(y) * gpost_ref[...])


def _ffn(x2d, mod3, g_pre, g_post, w_in, w_out, tm, rows_per_mod):
    rows = x2d.shape[0]
    r_mod = mod3.shape[1]
    tiles_per_mod = rows_per_mod // tm
    d_ff = w_out.shape[0]
    n_split = 2
    assert d_ff % (n_split * LANES) == 0

    def mod_spec(chunk):
        return pl.BlockSpec((1, r_mod, D_MODEL), lambda i: (i // tiles_per_mod, 0, chunk))

    def full(shape):
        return pl.BlockSpec(shape, lambda i: (0,) * len(shape))

    return pl.pallas_call(
        functools.partial(_ffn_kernel, d_ff=d_ff, n_split=n_split),
        out_shape=jax.ShapeDtypeStruct((rows, D_MODEL), F32),
        grid=(rows // tm,),
        in_specs=[pl.BlockSpec((tm, D_MODEL), lambda i: (i, 0)), mod_spec(3), mod_spec(4), mod_spec(5),
                  full((1, D_MODEL)), full((1, D_MODEL)), full((D_MODEL, 2 * d_ff)), full((d_ff, D_MODEL))],
        out_specs=pl.BlockSpec((tm, D_MODEL), lambda i: (i, 0)),
        compiler_params=pltpu.CompilerParams(dimension_semantics=("arbitrary",),
                                             vmem_limit_bytes=VMEM_LIMIT),
        name="ffn",
    )(x2d, mod3, mod3, mod3, g_pre, g_post, w_in, w_out)


def _pack_w_in(w):
    a = W_HEADS
    qa, ka, va, qi = w[:, 0:a], w[:, a:2 * a], w[:, 2 * a:3 * a], w[:, 3 * a:4 * a]
    o = 4 * a
    ki, wi = w[:, o:o + IDX_DIM], w[:, o + IDX_DIM:o + IDX_DIM + N_HEADS]
    o += IDX_DIM + N_HEADS
    qb, kb, vb = w[:, o:o + a], w[:, o + a:o + 2 * a], w[:, o + 2 * a:o + 3 * a]
    o += 3 * a
    ga, gb = w[:, o:o + D_MODEL], w[:, o + D_MODEL:o + 2 * D_MODEL]
    pad = jnp.zeros((w.shape[0], LANES - IDX_DIM - N_HEADS), w.dtype)
    slab_pad = jnp.zeros((w.shape[0], N_HEADS, SLAB - HEAD_DIM), w.dtype)
    vas = jnp.concatenate([va.reshape(-1, N_HEADS, HEAD_DIM), slab_pad], axis=2).reshape(-1, N_HEADS * SLAB)
    return jnp.concatenate([qa, ka, va, vas, qi, ki, wi, pad, qb, kb, vb, ga, gb], axis=1).astype(BF16)


def _rope_tables(pos):
    half = HEAD_DIM // 2
    freqs = ROPE_THETA ** (-jnp.arange(half, dtype=F32) / half)
    ang = pos.astype(F32)[:, None] * freqs[None, :]
    cos, sin = jnp.cos(ang), jnp.sin(ang)
    cos_h = jnp.concatenate([cos, cos], axis=1)
    sin_h = jnp.concatenate([-sin, sin], axis=1)
    cos2 = jnp.concatenate([cos_h, cos_h], axis=1)
    sin2 = jnp.concatenate([sin_h, sin_h], axis=1)
    cosk = jnp.concatenate([cos_h, jnp.ones_like(cos_h)], axis=1)
    sink = jnp.concatenate([sin_h, jnp.zeros_like(sin_h)], axis=1)
    return cos2, sin2, cosk, sink


def _token_minor_pages(cache):
    depth, n_pool = cache.shape[:2]
    return jnp.transpose(cache, (0, 1, 3, 4, 2)).reshape(depth, n_pool, W_HEADS, PAGE)


def kernel(x_prompt, x_sample, c_prompt, c_sample, cache_k_a, cache_v_a, cache_k_idx, cache_k_b, cache_v_b,
           page_table, w_ada, b_ada, g_pre_mix, g_post_mix, w_in, w_o_a, w_o_b, w_out, g_pre_ffn, g_post_ffn,
           w_ffn_in, w_ffn_out):
    depth = w_ada.shape[0]
    assert depth == 1
    bsz, seq, _ = x_prompt.shape
    dbs, dec_seq, _ = x_sample.shape
    assert dec_seq == 1 and dbs == LANES
    n_pages = page_table.shape[1]
    past = n_pages * PAGE
    assert past + dec_seq > TOPK_MAX
    layer = 0

    n_c = bsz + dbs
    c_all = jnp.concatenate([c_prompt, c_sample, jnp.zeros((-n_c % 8, D_MODEL), F32)], axis=0)
    mod = _modulation(c_all, w_ada[layer], b_ada[layer][None, :])
    mod_p = mod[:bsz].reshape(bsz, 1, 6 * D_MODEL)
    mod_s = mod[bsz:n_c].reshape(1, dbs, 6 * D_MODEL)

    w_packed = _pack_w_in(w_in[layer])
    woa, wob, wout = w_o_a[layer].astype(BF16), w_o_b[layer].astype(BF16), w_out[layer].astype(BF16)
    wfi, wfo = w_ffn_in[layer].astype(BF16), w_ffn_out[layer].astype(BF16)
    g1, g2 = g_pre_mix[layer][None, :], g_post_mix[layer][None, :]
    g3, g4 = g_pre_ffn[layer][None, :], g_post_ffn[layer][None, :]

    tm = 256
    xp = x_prompt.reshape(bsz * seq, D_MODEL)
    tabs_p = _rope_tables(jnp.arange(seq, dtype=I32))
    pp = _projection(xp, g1, mod_p, w_packed, tabs_p, tm, seq, seq // tm)
    oa_p = _dsa_prompt(pp, bsz, seq)
    ob_p = _sb_prompt(pp, bsz, seq)
    x1_p = _mix_out(xp, oa_p, ob_p, pp["ga"], pp["gb"], mod_p, woa, wob, wout, g2, tm, seq)
    y_p = _ffn(x1_p, mod_p, g3, g4, wfi, wfo, tm, seq).reshape(bsz, seq, D_MODEL)

    xs = x_sample.reshape(dbs, D_MODEL)
    tabs_s = _rope_tables(jnp.full((dbs,), past, dtype=I32))
    ps = _projection(xs, g1, mod_s, w_packed, tabs_s, dbs, dbs, 1)

    qi3 = ps["qi"].reshape(dbs, N_HEADS, IDX_DIM)
    w3 = ps["kiwf"][:, IDX_DIM:IDX_DIM + N_HEADS].reshape(dbs, N_HEADS, 1)
    kinew3 = ps["kib"][:, :IDX_DIM].reshape(dbs, 1, IDX_DIM)
    cki = jnp.transpose(cache_k_idx, (0, 1, 3, 2))
    scores = _idx_scores(page_table, qi3, w3, kinew3, cki)
    bias = _sample_select(scores.reshape(dbs, (n_pages + 8) * LANES), past + 1)
    bias3 = bias.reshape(dbs, n_pages + 8, LANES)

    oa_s = _dsa_sample(page_table, ps["qa"].reshape(dbs, 1, W_HEADS), ps["kat"][0], ps["vat"][0], bias3,
                       _token_minor_pages(cache_k_a), _token_minor_pages(cache_v_a)).reshape(dbs, W_HEADS)
    qbt = ps["qb"].astype(F32).T
    ob_s = _sb_sample(page_table, qbt, _token_minor_pages(cache_k_b)[layer],
                      _token_minor_pages(cache_v_b)[layer]).T

    x1_s = _mix_out(xs, oa_s, ob_s.astype(BF16), ps["ga"], ps["gb"], mod_s,
                    woa, wob, wout, g2, dbs, dbs)
    y_s = _ffn(x1_s, mod_s, g3, g4, wfi, wfo, dbs, dbs).reshape(dbs, 1, D_MODEL)

    def rows_p(t, width):
        if width == IDX_DIM:
            return jnp.transpose(t, (0, 2, 1)).reshape(depth, bsz, seq, IDX_DIM)
        return jnp.transpose(t.reshape(bsz, N_HEADS, HEAD_DIM, seq), (0, 3, 1, 2)).reshape(
            depth, bsz, seq, N_HEADS, HEAD_DIM)

    def rows_s(t, width):
        if width == IDX_DIM:
            return jnp.transpose(t[0], (1, 0)).reshape(depth, dbs, dec_seq, IDX_DIM)
        return jnp.transpose(t[0].reshape(N_HEADS, HEAD_DIM, dbs), (2, 0, 1)).reshape(
            depth, dbs, dec_seq, N_HEADS, HEAD_DIM)

    return (y_p, y_s,
            rows_p(pp["kat"], W_HEADS), rows_p(pp["vat"], W_HEADS), rows_p(pp["kit"], IDX_DIM),
            rows_p(pp["kbt"], W_HEADS), rows_p(pp["vbt"], W_HEADS),
            rows_s(ps["kat"], W_HEADS), rows_s(ps["vat"], W_HEADS), rows_s(ps["kit"], IDX_DIM),
            rows_s(ps["kbt"], W_HEADS), rows_s(ps["vbt"], W_HEADS))
```
